```python
import jax
import jax.numpy as jnp
from jax import lax
import numpy as np

D_MODEL = 4096
BATCH = 4
SEQ = 4096
DEPTH = 1

N_ATTN_HEADS = 16
HEAD_DIM = 128
ATTN_WIDTH = N_ATTN_HEADS * HEAD_DIM
N_CONV_GROUPS = 16
CONV_GROUP_DIM = 128
CONV_WIDTH = N_CONV_GROUPS * CONV_GROUP_DIM
CONV_K = 3
FFN_CONV_K = 3
D_FF = 7 * D_MODEL // 2
ROT_DIM = HEAD_DIM // 4
ROPE_THETA = 500000.0
MOBA_BLOCK = 256
MOBA_TOPK = 3
Q_CHUNK = 16
N_BRANCH = 2
N_MOD = 6
EPS = 1e-6
IN_PROJ_WIDTH = 3 * ATTN_WIDTH + 3 * CONV_WIDTH + N_BRANCH * D_MODEL

kernel_name = 'hybrid_moba_shortconv_convffn_adaln'


def rms_norm(x, g):
    xf = x.astype(jnp.float32)
    y = xf * lax.rsqrt(jnp.mean(xf * xf, axis=-1, keepdims=True) + EPS)
    return (y * g.astype(jnp.float32)).astype(x.dtype)


def modulate(h, shift, scale):
    return h * (1 + scale[:, None, :]) + shift[:, None, :]


def causal_dwconv(x, w):
    k = w.shape[0]
    return lax.conv_general_dilated(
        x, w[:, None, :].astype(x.dtype), window_strides=(1,), padding=[(k - 1, 0)],
        dimension_numbers=('NWC', 'WIO', 'NWC'), feature_group_count=x.shape[-1])


def partial_rope(x, positions):
    half = ROT_DIM // 2
    inv_freq = ROPE_THETA ** (-jnp.arange(0, ROT_DIM, 2, dtype=jnp.float32) / ROT_DIM)
    ang = positions.astype(jnp.float32)[:, None, :, None] * inv_freq
    cos, sin = jnp.cos(ang), jnp.sin(ang)
    xr = x[..., :ROT_DIM].astype(jnp.float32)
    x1, x2 = xr[..., :half], xr[..., half:]
    rot = jnp.concatenate([x1 * cos - x2 * sin, x2 * cos + x1 * sin], axis=-1).astype(x.dtype)
    return jnp.concatenate([rot, x[..., ROT_DIM:]], axis=-1)


def moba_attention(q, k, v):
    b, h, s, d = q.shape
    nb = -(-s // MOBA_BLOCK)
    s_pad = nb * MOBA_BLOCK
    pad = ((0, 0), (0, 0), (0, s_pad - s), (0, 0))
    q, k, v = jnp.pad(q, pad), jnp.pad(k, pad), jnp.pad(v, pad)
    scale = d ** -0.5
    k_blk = k.reshape(b, h, nb, MOBA_BLOCK, d)
    v_blk = v.reshape(b, h, nb, MOBA_BLOCK, d)
    k_mean = jnp.mean(k_blk.astype(jnp.float32), axis=3)
    q_pos = jnp.arange(s_pad)
    q_blk = q_pos // MOBA_BLOCK
    gate = jnp.einsum('bhsd,bhnd->bhsn', q.astype(jnp.float32), k_mean)
    fully_past = jnp.arange(nb)[None, :] < q_blk[:, None]
    gate = jnp.where(fully_past, gate, -jnp.inf)
    n_sel = max(min(MOBA_TOPK, nb - 1), 1)
    _, sel = lax.top_k(gate, n_sel)
    sel_valid = sel < q_blk[:, None]
    nc = s_pad // Q_CHUNK

    def to_chunks(t):
        t = t.reshape(b, h, nc, Q_CHUNK, *t.shape[3:])
        return jnp.moveaxis(t, 2, 0)

    bi = jnp.arange(b)[:, None, None]
    hi = jnp.arange(h)[None, :, None]
    kpos_local = jnp.arange(MOBA_BLOCK)

    def attend_chunk(args):
        q_c, sel_c, valid_c, pos_c = args
        start = (pos_c[0] // MOBA_BLOCK) * MOBA_BLOCK
        k_own = lax.dynamic_slice_in_dim(k, start, MOBA_BLOCK, axis=2)
        v_own = lax.dynamic_slice_in_dim(v, start, MOBA_BLOCK, axis=2)
        s_own = jnp.einsum('bhqd,bhkd->bhqk', q_c, k_own).astype(jnp.float32) * scale
        s_own = jnp.where((start + kpos_local)[None, :] <= pos_c[:, None], s_own, -jnp.inf)
        sel_flat = sel_c.reshape(b, h, Q_CHUNK * n_sel)
        k_sel = k_blk[bi, hi, sel_flat].reshape(b, h, Q_CHUNK, n_sel * MOBA_BLOCK, d)
        v_sel = v_blk[bi, hi, sel_flat].reshape(b, h, Q_CHUNK, n_sel * MOBA_BLOCK, d)
        s_sel = jnp.einsum('bhqd,bhqkd->bhqk', q_c, k_sel).astype(jnp.float32) * scale
        s_sel = jnp.where(jnp.repeat(valid_c, MOBA_BLOCK, axis=-1), s_sel, -jnp.inf)
        p = jax.nn.softmax(jnp.concatenate([s_own, s_sel], axis=-1), axis=-1).astype(v.dtype)
        return (jnp.einsum('bhqk,bhkd->bhqd', p[..., :MOBA_BLOCK], v_own)
                + jnp.einsum('bhqk,bhqkd->bhqd', p[..., MOBA_BLOCK:], v_sel))

    out = lax.map(attend_chunk, (to_chunks(q), to_chunks(sel), to_chunks(sel_valid),
                                 q_pos.reshape(nc, Q_CHUNK)))
    out = jnp.moveaxis(out, 0, 2).reshape(b, h, s_pad, d)
    return out[:, :, :s]


def setup_inputs(seed: int = 0) -> dict:
    key = jax.random.key(seed)
    ks = jax.random.split(key, 16)
    f32 = jnp.float32
    nrm = lambda k, shape, s: jax.random.normal(k, shape, f32) * s
    return {
        'x': nrm(ks[0], (BATCH, SEQ, D_MODEL), 1.0),
        'c': nrm(ks[1], (BATCH, D_MODEL), 1.0),
        'positions': jnp.broadcast_to(jnp.arange(SEQ, dtype=jnp.int32)[None, :], (BATCH, SEQ)),
        'w_ada': nrm(ks[2], (DEPTH, D_MODEL, N_MOD * D_MODEL), 0.5 * D_MODEL ** -0.5),
        'b_ada': nrm(ks[3], (DEPTH, N_MOD * D_MODEL), 0.01),
        'g_mix': 1.0 + nrm(ks[4], (DEPTH, D_MODEL), 0.05),
        'w_in': nrm(ks[5], (DEPTH, D_MODEL, IN_PROJ_WIDTH), D_MODEL ** -0.5),
        'conv_w': nrm(ks[6], (DEPTH, CONV_K, CONV_WIDTH), CONV_K ** -0.5),
        'w_attn_out': nrm(ks[7], (DEPTH, ATTN_WIDTH, D_MODEL), ATTN_WIDTH ** -0.5),
        'w_conv_out': nrm(ks[8], (DEPTH, CONV_WIDTH, D_MODEL), CONV_WIDTH ** -0.5),
        'w_o': nrm(ks[9], (DEPTH, D_MODEL, D_MODEL), D_MODEL ** -0.5),
        'g_ffn': 1.0 + nrm(ks[10], (DEPTH, D_MODEL), 0.05),
        'w_up': nrm(ks[11], (DEPTH, D_MODEL, 2 * D_FF), D_MODEL ** -0.5),
        'ffn_conv_w': nrm(ks[12], (DEPTH, FFN_CONV_K, 2 * D_FF), FFN_CONV_K ** -0.5),
        'w_down': nrm(ks[13], (DEPTH, D_FF, D_MODEL), D_FF ** -0.5),
        'g_final': 1.0 + nrm(ks[14], (D_MODEL,), 0.05),
    }


def reference(x, c, positions, w_ada, b_ada, g_mix, w_in, conv_w, w_attn_out, w_conv_out,
              w_o, g_ffn, w_up, ffn_conv_w, w_down, g_final):
    b, s, _ = x.shape
    a, cw = ATTN_WIDTH, CONV_WIDTH
    split_at = [a, 2 * a, 3 * a, 3 * a + cw, 3 * a + 2 * cw, 3 * a + 3 * cw, 3 * a + 3 * cw + D_MODEL]
    c_act = jax.nn.silu(c)
    for layer in range(DEPTH):
        mod = c_act @ w_ada[layer] + b_ada[layer]
        shift1, scale1, gate1, shift2, scale2, gate2 = jnp.split(mod, N_MOD, axis=-1)

        h = modulate(rms_norm(x, g_mix[layer]), shift1, scale1)
        proj = h @ w_in[layer]
        q, k, v, cb, cc, cu, g_att, g_cnv = jnp.split(proj, split_at, axis=-1)

        to_heads = lambda t: t.reshape(b, s, N_ATTN_HEADS, HEAD_DIM).transpose(0, 2, 1, 3)
        qh = partial_rope(to_heads(q), positions)
        kh = partial_rope(to_heads(k), positions)
        attn = moba_attention(qh, kh, to_heads(v))
        y_att = attn.transpose(0, 2, 1, 3).reshape(b, s, ATTN_WIDTH) @ w_attn_out[layer]

        y_cnv = (cb * causal_dwconv(cc * cu, conv_w[layer])) @ w_conv_out[layer]

        merged = jax.nn.sigmoid(g_att) * y_att + jax.nn.sigmoid(g_cnv) * y_cnv
        x = x + gate1[:, None, :] * (merged @ w_o[layer])

        h2 = modulate(rms_norm(x, g_ffn[layer]), shift2, scale2)
        up = causal_dwconv(h2 @ w_up[layer], ffn_conv_w[layer])
        u_gate, u_val = jnp.split(up, 2, axis=-1)
        x = x + gate2[:, None, :] * ((jax.nn.silu(u_gate) * u_val) @ w_down[layer])
    return rms_norm(x, g_final)
```

```python
import functools

import jax
import jax.numpy as jnp
from jax import lax
from jax.experimental import pallas as pl
from jax.experimental.pallas import tpu as pltpu

F32 = jnp.float32
BF16 = jnp.bfloat16

HEAD_DIM = 128
ROT_DIM = HEAD_DIM // 4
ROPE_THETA = 500000.0
MOBA_BLOCK = 256
MOBA_TOPK = 3
EPS = 1e-6
N_MOD = 6
LANES = 128
HALO = 8
VMEM_LIMIT = 56 * 1024 * 1024


def _tile(dim, target):
    if dim <= target:
        return dim
    t = (target // LANES) * LANES
    while t >= LANES:
        if dim % t == 0:
            return t
        t -= LANES
    return dim


def _params(sem):
    return pltpu.CompilerParams(dimension_semantics=sem, vmem_limit_bytes=VMEM_LIMIT)


def _ada_kernel(c_ref, w_ref, b_ref, o_ref):
    c = c_ref[...]
    c_act = c * (1.0 / (1.0 + jnp.exp(-c)))
    acc = jnp.dot(c_act.astype(BF16), w_ref[...].astype(BF16), preferred_element_type=F32)
    o_ref[...] = acc + b_ref[...]


def _ada(c, w, bias):
    b, d = c.shape
    n = w.shape[1]
    tn = _tile(n, 512)
    return pl.pallas_call(
        _ada_kernel,
        grid=(n // tn,),
        in_specs=[pl.BlockSpec((b, d), lambda j: (0, 0)),
                  pl.BlockSpec((d, tn), lambda j: (0, j)),
                  pl.BlockSpec((1, tn), lambda j: (0, j))],
        out_specs=pl.BlockSpec((b, tn), lambda j: (0, j)),
        out_shape=jax.ShapeDtypeStruct((b, n), F32),
        compiler_params=_params(("arbitrary",)),
        name="ada",
    )(c, w, bias.reshape(1, n))


def _rope_kernel(pos_ref, invf_ref, sign_ref, cos_ref, sin_ref):
    ang = pos_ref[...].astype(F32) * invf_ref[...]
    cos_ref[...] = jnp.cos(ang)
    sin_ref[...] = jnp.sin(ang) * sign_ref[...]


def _rope_tables(positions):
    t = positions.size
    half = ROT_DIM // 2
    inv_freq = ROPE_THETA ** (-jnp.arange(0, ROT_DIM, 2, dtype=F32) / ROT_DIM)
    invf = jnp.zeros((HEAD_DIM,), F32).at[:ROT_DIM].set(jnp.tile(inv_freq, 2)).reshape(1, HEAD_DIM)
    sign = jnp.ones((HEAD_DIM,), F32).at[:half].set(-1.0).reshape(1, HEAD_DIM)
    tm = _tile(t, 2048)
    return pl.pallas_call(
        _rope_kernel,
        grid=(t // tm,),
        in_specs=[pl.BlockSpec((tm, 1), lambda i: (i, 0)),
                  pl.BlockSpec((1, HEAD_DIM), lambda i: (0, 0)),
                  pl.BlockSpec((1, HEAD_DIM), lambda i: (0, 0))],
        out_specs=[pl.BlockSpec((tm, HEAD_DIM), lambda i: (i, 0)),
                   pl.BlockSpec((tm, HEAD_DIM), lambda i: (i, 0))],
        out_shape=[jax.ShapeDtypeStruct((t, HEAD_DIM), F32)] * 2,
        compiler_params=_params(("arbitrary",)),
        name="rope_tables",
    )(positions.reshape(t, 1), invf, sign)


def _normmod_kernel(x_ref, g_ref, shift_ref, scale_ref, o_ref):
    x = x_ref[...]
    y = x * lax.rsqrt(jnp.mean(x * x, axis=-1, keepdims=True) + EPS) * g_ref[...]
    o_ref[...] = (y * (1.0 + scale_ref[0]) + shift_ref[0]).astype(o_ref.dtype)


def _normmod(x2d, g, mod3, shift_idx, scale_idx, seq):
    t, d = x2d.shape
    tm = _tile(seq, 256)
    per = seq // tm
    return pl.pallas_call(
        _normmod_kernel,
        grid=(t // tm,),
        in_specs=[pl.BlockSpec((tm, d), lambda i: (i, 0)),
                  pl.BlockSpec((1, d), lambda i: (0, 0)),
                  pl.BlockSpec((1, 1, d), lambda i: ((i // per) * N_MOD + shift_idx, 0, 0)),
                  pl.BlockSpec((1, 1, d), lambda i: ((i // per) * N_MOD + scale_idx, 0, 0))],
        out_specs=pl.BlockSpec((tm, d), lambda i: (i, 0)),
        out_shape=jax.ShapeDtypeStruct((t, d), BF16),
        compiler_params=_params(("arbitrary",)),
        name="normmod",
    )(x2d, g.reshape(1, d), mod3, mod3)


def _final_norm_kernel(x_ref, g_ref, o_ref):
    x = x_ref[...]
    o_ref[...] = x * lax.rsqrt(jnp.mean(x * x, axis=-1, keepdims=True) + EPS) * g_ref[...]


def _final_norm(x2d, g):
    t, d = x2d.shape
    tm = _tile(t, 256)
    return pl.pallas_call(
        _final_norm_kernel,
        grid=(t // tm,),
        in_specs=[pl.BlockSpec((tm, d), lambda i: (i, 0)),
                  pl.BlockSpec((1, d), lambda i: (0, 0))],
        out_specs=pl.BlockSpec((tm, d), lambda i: (i, 0)),
        out_shape=jax.ShapeDtypeStruct((t, d), F32),
        compiler_params=_params(("arbitrary",)),
        name="final_norm",
    )(x2d, g.reshape(1, d))


def _qkv_kernel(h_ref, w_ref, cos_ref, sin_ref, o_ref, *, rope_tiles):
    acc = jnp.dot(h_ref[...], w_ref[...], preferred_element_type=F32)
    tm, tn = acc.shape
    j = pl.program_id(0)

    @pl.when(j < rope_tiles)
    def _():
        reps = tn // HEAD_DIM
        cos = jnp.tile(cos_ref[...], (1, reps))
        sin = jnp.tile(sin_ref[...], (1, reps))
        lane = lax.broadcasted_iota(jnp.int32, (tm, tn), 1) % HEAD_DIM
        partner = jnp.where(lane < ROT_DIM // 2,
                            pltpu.roll(acc, tn - ROT_DIM // 2, 1),
                            pltpu.roll(acc, ROT_DIM // 2, 1))
        o_ref[...] = (acc * cos + partner * sin).astype(o_ref.dtype)

    @pl.when(j >= rope_tiles)
    def _():
        o_ref[...] = acc.astype(o_ref.dtype)


def _qkv(h, w_in, cos_t, sin_t, attn_width):
    t, d = h.shape
    n = 3 * attn_width
    tm = _tile(t, 1024)
    tn = _tile(attn_width, 1024)
    return pl.pallas_call(
        functools.partial(_qkv_kernel, rope_tiles=2 * attn_width // tn),
        grid=(n // tn, t // tm),
        in_specs=[pl.BlockSpec((tm, d), lambda j, i: (i, 0)),
                  pl.BlockSpec((d, tn), lambda j, i: (0, j)),
                  pl.BlockSpec((tm, HEAD_DIM), lambda j, i: (i, 0)),
                  pl.BlockSpec((tm, HEAD_DIM), lambda j, i: (i, 0))],
        out_specs=pl.BlockSpec((tm, tn), lambda j, i: (i, j)),
        out_shape=jax.ShapeDtypeStruct((t, n), BF16),
        compiler_params=_params(("arbitrary", "arbitrary")),
        name="qkv_rope",
    )(h, w_in, cos_t, sin_t)


def _causal_conv3(u, w_ref, hist_ref, first_of_seq):
    tm = u.shape[0]

    @pl.when(first_of_seq)
    def _():
        hist_ref[0:HALO, :] = jnp.zeros((HALO, u.shape[1]), F32)

    hist_ref[HALO:HALO + tm, :] = u
    u1 = hist_ref[HALO - 1:HALO - 1 + tm, :]
    u2 = hist_ref[HALO - 2:HALO - 2 + tm, :]
    y = u * w_ref[2:3, :] + u1 * w_ref[1:2, :] + u2 * w_ref[0:1, :]
    hist_ref[0:HALO, :] = hist_ref[tm:tm + HALO, :]
    return y


def _convbranch_kernel(h_ref, wb_ref, wc_ref, wu_ref, cw_ref, o_ref, hist_ref, *, per_seq):
    h = h_ref[...]
    cb = jnp.dot(h, wb_ref[...], preferred_element_type=F32)
    cc = jnp.dot(h, wc_ref[...], preferred_element_type=F32)
    cu = jnp.dot(h, wu_ref[...], preferred_element_type=F32)
    first = pl.program_id(1) % per_seq == 0
    y = _causal_conv3(cc * cu, cw_ref, hist_ref, first)
    o_ref[...] = (cb * y).astype(o_ref.dtype)


def _convbranch(h, w_in, conv_w, col0, conv_width, seq):
    t, d = h.shape
    tm = _tile(seq, 1024)
    tn = _tile(conv_width, 512)
    nj = conv_width // tn
    b0, c0, u0 = col0 // tn, (col0 + conv_width) // tn, (col0 + 2 * conv_width) // tn
    return pl.pallas_call(
        functools.partial(_convbranch_kernel, per_seq=seq // tm),
        grid=(nj, t // tm),
        in_specs=[pl.BlockSpec((tm, d), lambda j, i: (i, 0)),
                  pl.BlockSpec((d, tn), lambda j, i: (0, b0 + j)),
                  pl.BlockSpec((d, tn), lambda j, i: (0, c0 + j)),
                  pl.BlockSpec((d, tn), lambda j, i: (0, u0 + j)),
                  pl.BlockSpec((conv_w.shape[0], tn), lambda j, i: (0, j))],
        out_specs=pl.BlockSpec((tm, tn), lambda j, i: (i, j)),
        out_shape=jax.ShapeDtypeStruct((t, conv_width), BF16),
        scratch_shapes=[pltpu.VMEM((HALO + tm, tn), F32)],
        compiler_params=_params(("arbitrary", "arbitrary")),
        name="conv_branch",
    )(h, w_in, w_in, w_in, conv_w)


def _gates_kernel(h_ref, w_ref, o_ref):
    g = jnp.dot(h_ref[...], w_ref[...], preferred_element_type=F32)
    o_ref[...] = (1.0 / (1.0 + jnp.exp(-g))).astype(o_ref.dtype)


def _gates(h, w_in, col0, width):
    t, d = h.shape
    tm = _tile(t, 1024)
    tn = _tile(width, 1024)
    j0 = col0 // tn
    return pl.pallas_call(
        _gates_kernel,
        grid=(width // tn, t // tm),
        in_specs=[pl.BlockSpec((tm, d), lambda j, i: (i, 0)),
                  pl.BlockSpec((d, tn), lambda j, i: (0, j0 + j))],
        out_specs=pl.BlockSpec((tm, tn), lambda j, i: (i, j)),
        out_shape=jax.ShapeDtypeStruct((t, width), BF16),
        compiler_params=_params(("arbitrary", "arbitrary")),
        name="gates",
    )(h, w_in)


def _moba_kernel(q_ref, k_ref, v_ref, o_ref, kmean_ref, sel_ref, *, nb, scale):
    i = pl.program_id(2)
    blk = MOBA_BLOCK
    neg_inf = -jnp.inf

    @pl.when(i == 0)
    def _():
        kf = k_ref[...].astype(F32).reshape(nb, blk, HEAD_DIM)
        kmean_ref[...] = jnp.sum(kf, axis=1) * (1.0 / blk)

    q = q_ref[...]
    gate = lax.dot_general(kmean_ref[...], q.astype(F32), (((1,), (1,)), ((), ())),
                           precision=lax.Precision.HIGHEST, preferred_element_type=F32)
    row = lax.broadcasted_iota(jnp.int32, (nb, blk), 0)
    gate = jnp.where(row < i, gate, neg_inf)
    sel = jnp.zeros((nb, blk), F32)
    for _ in range(MOBA_TOPK):
        best = jnp.max(gate, axis=0, keepdims=True)
        hit = (gate == best) & (best > neg_inf)
        first = jnp.min(jnp.where(hit, row, nb), axis=0, keepdims=True)
        pick = row == first
        sel = jnp.where(pick, 1.0, sel)
        gate = jnp.where(pick, neg_inf, gate)
    sel_ref[...] = sel

    def scores(j):
        kj = k_ref[pl.ds(pl.multiple_of(j * blk, blk), blk), :]
        return lax.dot_general(kj, q, (((1,), (1,)), ((), ())),
                               preferred_element_type=F32) * scale

    def pv(j, p):
        vj = v_ref[pl.ds(pl.multiple_of(j * blk, blk), blk), :]
        return lax.dot_general(vj, p.astype(vj.dtype), (((0,), (0,)), ((), ())),
                               preferred_element_type=F32)

    kpos = lax.broadcasted_iota(jnp.int32, (blk, blk), 0)
    qpos = lax.broadcasted_iota(jnp.int32, (blk, blk), 1)
    s = jnp.where(kpos <= qpos, scores(i), neg_inf)
    m0 = jnp.max(s, axis=0, keepdims=True)
    p = jnp.exp(s - m0)
    l0 = jnp.sum(p, axis=0, keepdims=True)
    acc0 = pv(i, p)

    def body(j, carry):
        m, l, acc = carry
        chosen = sel_ref[pl.ds(j, 1), :] > 0.0
        s = jnp.where(chosen, scores(j), neg_inf)
        m_new = jnp.maximum(m, jnp.max(s, axis=0, keepdims=True))
        alpha = jnp.exp(m - m_new)
        p = jnp.exp(s - m_new)
        l = alpha * l + jnp.sum(p, axis=0, keepdims=True)
        acc = alpha * acc + pv(j, p)
        return m_new, l, acc

    _, l, acc = lax.fori_loop(0, i, body, (m0, l0, acc0))
    o_ref[...] = (acc / l).T.astype(o_ref.dtype)


def _moba(qkv, batch, seq, n_heads):
    t = qkv.shape[0]
    nb = seq // MOBA_BLOCK
    kernel = functools.partial(_moba_kernel, nb=nb, scale=HEAD_DIM ** -0.5)
    return pl.pallas_call(
        kernel,
        grid=(batch, n_heads, nb),
        in_specs=[pl.BlockSpec((MOBA_BLOCK, HEAD_DIM), lambda b, h, i: (b * nb + i, h)),
                  pl.BlockSpec((seq, HEAD_DIM), lambda b, h, i: (b, n_heads + h)),
                  pl.BlockSpec((seq, HEAD_DIM), lambda b, h, i: (b, 2 * n_heads + h))],
        out_specs=pl.BlockSpec((MOBA_BLOCK, HEAD_DIM), lambda b, h, i: (b * nb + i, h)),
        out_shape=jax.ShapeDtypeStruct((t, n_heads * HEAD_DIM), BF16),
        scratch_shapes=[pltpu.VMEM((nb, HEAD_DIM), F32),
                        pltpu.VMEM((nb, MOBA_BLOCK), F32)],
        compiler_params=_params(("arbitrary", "arbitrary", "arbitrary")),
        name="moba",
    )(qkv, qkv, qkv)


def _merge_kernel(a_ref, z_ref, wa_ref, wz_ref, ga_ref, gz_ref, o_ref):
    ya = jnp.dot(a_ref[...], wa_ref[...], preferred_element_type=F32)
    yz = jnp.dot(z_ref[...], wz_ref[...], preferred_element_type=F32)
    o_ref[...] = (ga_ref[...].astype(F32) * ya + gz_ref[...].astype(F32) * yz).astype(o_ref.dtype)


def _merge(attn, z, w_attn_out, w_conv_out, gates):
    t, ka = attn.shape
    kz = z.shape[1]
    n = w_attn_out.shape[1]
    tm = _tile(t, 1024)
    tn = _tile(n, 1024)
    nj = n // tn
    return pl.pallas_call(
        _merge_kernel,
        grid=(nj, t // tm),
        in_specs=[pl.BlockSpec((tm, ka), lambda j, i: (i, 0)),
                  pl.BlockSpec((tm, kz), lambda j, i: (i, 0)),
                  pl.BlockSpec((ka, tn), lambda j, i: (0, j)),
                  pl.BlockSpec((kz, tn), lambda j, i: (0, j)),
                  pl.BlockSpec((tm, tn), lambda j, i: (i, j)),
                  pl.BlockSpec((tm, tn), lambda j, i: (i, nj + j))],
        out_specs=pl.BlockSpec((tm, tn), lambda j, i: (i, j)),
        out_shape=jax.ShapeDtypeStruct((t, n), BF16),
        compiler_params=_params(("arbitrary", "arbitrary")),
        name="merge",
    )(attn, z, w_attn_out, w_conv_out, gates, gates)


def _resid_kernel(a_ref, w_ref, x_ref, g_ref, o_ref, acc_ref):
    k = pl.program_id(2)
    part = jnp.dot(a_ref[...], w_ref[...], preferred_element_type=F32)

    @pl.when(k == 0)
    def _():
        acc_ref[...] = part

    @pl.when(k > 0)
    def _():
        acc_ref[...] += part

    @pl.when(k == pl.num_programs(2) - 1)
    def _():
        o_ref[...] = x_ref[...] + g_ref[0] * acc_ref[...]


def _resid(a, w, x2d, mod3, gate_idx, seq, tk_target):
    t, kdim = a.shape
    n = w.shape[1]
    tm = _tile(seq, 1024)
    tn = _tile(n, 1024)
    tk = _tile(kdim, tk_target)
    per = seq // tm
    return pl.pallas_call(
        _resid_kernel,
        grid=(n // tn, t // tm, kdim // tk),
        in_specs=[pl.BlockSpec((tm, tk), lambda j, i, k: (i, k)),
                  pl.BlockSpec((tk, tn), lambda j, i, k: (k, j)),
                  pl.BlockSpec((tm, tn), lambda j, i, k: (i, j)),
                  pl.BlockSpec((1, 1, tn), lambda j, i, k: ((i // per) * N_MOD + gate_idx, 0, j))],
        out_specs=pl.BlockSpec((tm, tn), lambda j, i, k: (i, j)),
        out_shape=jax.ShapeDtypeStruct((t, n), F32),
        scratch_shapes=[pltpu.VMEM((tm, tn), F32)],
        compiler_params=_params(("arbitrary", "arbitrary", "arbitrary")),
        name="resid_matmul",
    )(a, w, x2d, mod3)


def _ffn_up_kernel(h_ref, wg_ref, wv_ref, cg_ref, cv_ref, o_ref, hg_ref, hv_ref, *, per_seq):
    h = h_ref[...]
    first = pl.program_id(1) % per_seq == 0
    ug = _causal_conv3(jnp.dot(h, wg_ref[...], preferred_element_type=F32), cg_ref, hg_ref, first)
    uv = _causal_conv3(jnp.dot(h, wv_ref[...], preferred_element_type=F32), cv_ref, hv_ref, first)
    o_ref[...] = (ug * (1.0 / (1.0 + jnp.exp(-ug))) * uv).astype(o_ref.dtype)


def _ffn_up(h2, w_up, conv_w, d_ff, seq):
    t, d = h2.shape
    tm = _tile(seq, 1024)
    tn = _tile(d_ff, 512)
    nj = d_ff // tn
    kw = conv_w.shape[0]
    return pl.pallas_call(
        functools.partial(_ffn_up_kernel, per_seq=seq // tm),
        grid=(nj, t // tm),
        in_specs=[pl.BlockSpec((tm, d), lambda j, i: (i, 0)),
                  pl.BlockSpec((d, tn), lambda j, i: (0, j)),
                  pl.BlockSpec((d, tn), lambda j, i: (0, nj + j)),
                  pl.BlockSpec((kw, tn), lambda j, i: (0, j)),
                  pl.BlockSpec((kw, tn), lambda j, i: (0, nj + j))],
        out_specs=pl.BlockSpec((tm, tn), lambda j, i: (i, j)),
        out_shape=jax.ShapeDtypeStruct((t, d_ff), BF16),
        scratch_shapes=[pltpu.VMEM((HALO + tm, tn), F32),
                        pltpu.VMEM((HALO + tm, tn), F32)],
        compiler_params=_params(("arbitrary", "arbitrary")),
        name="ffn_up",
    )(h2, w_up, w_up, conv_w, conv_w)


def kernel(x, c, positions, w_ada, b_ada, g_mix, w_in, conv_w, w_attn_out, w_conv_out,
           w_o, g_ffn, w_up, ffn_conv_w, w_down, g_final):
    b, s, d = x.shape
    depth = w_ada.shape[0]
    a = w_attn_out.shape[1]
    cw = w_conv_out.shape[1]
    d_ff = w_down.shape[1]
    n_heads = a // HEAD_DIM
    assert conv_w.shape[1] == 3 and ffn_conv_w.shape[1] == 3
    assert s % MOBA_BLOCK == 0 and w_in.shape[2] == 3 * a + 3 * cw + 2 * d

    xt = x.reshape(b * s, d)
    cos_t, sin_t = _rope_tables(positions)
    for layer in range(depth):
        mod3 = _ada(c, w_ada[layer], b_ada[layer]).reshape(b * N_MOD, 1, d)
        w_in_l = w_in[layer].astype(BF16)

        h = _normmod(xt, g_mix[layer], mod3, 0, 1, s)
        qkv = _qkv(h, w_in_l, cos_t, sin_t, a)
        z = _convbranch(h, w_in_l, conv_w[layer], 3 * a, cw, s)
        gates = _gates(h, w_in_l, 3 * a + 3 * cw, 2 * d)
        attn = _moba(qkv, b, s, n_heads)
        merged = _merge(attn, z, w_attn_out[layer].astype(BF16), w_conv_out[layer].astype(BF16), gates)
        xt = _resid(merged, w_o[layer].astype(BF16), xt, mod3, 2, s, 2048)

        h2 = _normmod(xt, g_ffn[layer], mod3, 3, 4, s)
        act = _ffn_up(h2, w_up[layer].astype(BF16), ffn_conv_w[layer], d_ff, s)
        xt = _resid(act, w_down[layer].astype(BF16), xt, mod3, 5, s, 2048)
    return _final_norm(xt, g_final).reshape(b, s, d)
```

```python
import functools
import math

import jax
import jax.numpy as jnp
from jax import lax
from jax.experimental import pallas as pl
from jax.experimental.pallas import tpu as pltpu

F32 = jnp.float32
BF16 = jnp.bfloat16

HEAD_DIM = 128
ROT_DIM = HEAD_DIM // 4
ROPE_THETA = 500000.0
MOBA_BLOCK = 256
MOBA_TOPK = 3
EPS = 1e-6
N_MOD = 6
LANES = 128
MXU_COLS = 256
ROW_CHUNK = 256
HALO = 8
VMEM_LIMIT = 56 * 1024 * 1024
LOG2E = math.log2(math.e)


def _tile(dim, target):
    if dim <= target:
        return dim
    t = (target // LANES) * LANES
    while t >= LANES:
        if dim % t == 0:
            return t
        t -= LANES
    return dim


def _chunks(tn):
    w = MXU_COLS if tn % MXU_COLS == 0 else tn
    return [slice(c * w, (c + 1) * w) for c in range(tn // w)]


def _row_chunks(tm):
    r = ROW_CHUNK if tm % ROW_CHUNK == 0 else tm
    return [slice(k * r, (k + 1) * r) for k in range(tm // r)]


def _params(sem):
    return pltpu.CompilerParams(dimension_semantics=sem, vmem_limit_bytes=VMEM_LIMIT)


def _sigmoid(x):
    return 1.0 / (1.0 + jnp.exp(-x))


def _ada_kernel(c_ref, w_ref, b_ref, o_ref):
    c = c_ref[...]
    c_act = c * _sigmoid(c)
    acc = jnp.dot(c_act.astype(BF16), w_ref[...].astype(BF16), preferred_element_type=F32)
    o_ref[...] = acc + b_ref[...]


def _ada(c, w, bias):
    b, d = c.shape
    n = w.shape[1]
    tn = _tile(n, 512)
    return pl.pallas_call(
        _ada_kernel,
        grid=(n // tn,),
        in_specs=[pl.BlockSpec((b, d), lambda j: (0, 0)),
                  pl.BlockSpec((d, tn), lambda j: (0, j)),
                  pl.BlockSpec((1, tn), lambda j: (0, j))],
        out_specs=pl.BlockSpec((b, tn), lambda j: (0, j)),
        out_shape=jax.ShapeDtypeStruct((b, n), F32),
        compiler_params=_params(("arbitrary",)),
        name="ada",
    )(c, w, bias.reshape(1, n))


def _rope_kernel(pos_ref, invf_ref, sign_ref, cos_ref, sin_ref):
    ang = pos_ref[...].astype(F32) * invf_ref[...]
    cos_ref[...] = jnp.cos(ang)
    sin_ref[...] = jnp.sin(ang) * sign_ref[...]


def _rope_tables(positions):
    t = positions.size
    half = ROT_DIM // 2
    inv_freq = ROPE_THETA ** (-jnp.arange(0, ROT_DIM, 2, dtype=F32) / ROT_DIM)
    invf = jnp.zeros((HEAD_DIM,), F32).at[:ROT_DIM].set(jnp.tile(inv_freq, 2)).reshape(1, HEAD_DIM)
    sign = jnp.ones((HEAD_DIM,), F32).at[:half].set(-1.0).reshape(1, HEAD_DIM)
    tm = _tile(t, 2048)
    return pl.pallas_call(
        _rope_kernel,
        grid=(t // tm,),
        in_specs=[pl.BlockSpec((tm, 1), lambda i: (i, 0)),
                  pl.BlockSpec((1, HEAD_DIM), lambda i: (0, 0)),
                  pl.BlockSpec((1, HEAD_DIM), lambda i: (0, 0))],
        out_specs=[pl.BlockSpec((tm, HEAD_DIM), lambda i: (i, 0)),
                   pl.BlockSpec((tm, HEAD_DIM), lambda i: (i, 0))],
        out_shape=[jax.ShapeDtypeStruct((t, HEAD_DIM), F32)] * 2,
        compiler_params=_params(("arbitrary",)),
        name="rope_tables",
    )(positions.reshape(t, 1), invf, sign)


def _normmod_kernel(x_ref, g_ref, shift_ref, scale_ref, o_ref):
    x = x_ref[...]
    y = x * lax.rsqrt(jnp.mean(x * x, axis=-1, keepdims=True) + EPS) * g_ref[...]
    o_ref[...] = (y * (1.0 + scale_ref[0]) + shift_ref[0]).astype(o_ref.dtype)


def _normmod(x2d, g, mod3, shift_idx, scale_idx, seq):
    t, d = x2d.shape
    tm = _tile(seq, 256)
    per = seq // tm
    return pl.pallas_call(
        _normmod_kernel,
        grid=(t // tm,),
        in_specs=[pl.BlockSpec((tm, d), lambda i: (i, 0)),
                  pl.BlockSpec((1, d), lambda i: (0, 0)),
                  pl.BlockSpec((1, 1, d), lambda i: ((i // per) * N_MOD + shift_idx, 0, 0)),
                  pl.BlockSpec((1, 1, d), lambda i: ((i // per) * N_MOD + scale_idx, 0, 0))],
        out_specs=pl.BlockSpec((tm, d), lambda i: (i, 0)),
        out_shape=jax.ShapeDtypeStruct((t, d), BF16),
        compiler_params=_params(("arbitrary",)),
        name="normmod",
    )(x2d, g.reshape(1, d), mod3, mod3)


def _final_norm_kernel(x_ref, g_ref, o_ref):
    x = x_ref[...]
    o_ref[...] = x * lax.rsqrt(jnp.mean(x * x, axis=-1, keepdims=True) + EPS) * g_ref[...]


def _final_norm(x2d, g):
    t, d = x2d.shape
    tm = _tile(t, 256)
    return pl.pallas_call(
        _final_norm_kernel,
        grid=(t // tm,),
        in_specs=[pl.BlockSpec((tm, d), lambda i: (i, 0)),
                  pl.BlockSpec((1, d), lambda i: (0, 0))],
        out_specs=pl.BlockSpec((tm, d), lambda i: (i, 0)),
        out_shape=jax.ShapeDtypeStruct((t, d), F32),
        compiler_params=_params(("arbitrary",)),
        name="final_norm",
    )(x2d, g.reshape(1, d))


def _qk_kernel(h_ref, w_ref, cos_ref, sin_ref, o_ref, *, q_tiles, q_scale):
    fac = jnp.where(pl.program_id(0) < q_tiles, q_scale, 1.0).astype(F32)
    half = ROT_DIM // 2
    for rows in _row_chunks(h_ref.shape[0]):
        h = h_ref[rows, :]
        cos = cos_ref[rows, :] * fac
        sin = sin_ref[rows, :] * fac
        for cols in _chunks(o_ref.shape[1]):
            acc = jnp.dot(h, w_ref[:, cols], preferred_element_type=F32)
            wc = acc.shape[1]
            reps = wc // HEAD_DIM
            lane = lax.broadcasted_iota(jnp.int32, acc.shape, 1) % HEAD_DIM
            partner = jnp.where(lane < half, pltpu.roll(acc, wc - half, 1),
                                pltpu.roll(acc, half, 1))
            o_ref[rows, cols] = (acc * jnp.tile(cos, (1, reps))
                                 + partner * jnp.tile(sin, (1, reps))).astype(o_ref.dtype)


def _qk(h, w_in, cos_t, sin_t, attn_width):
    t, d = h.shape
    n = 2 * attn_width
    tm = _tile(t, 1024)
    tn = _tile(attn_width, 1024)
    kernel = functools.partial(_qk_kernel, q_tiles=attn_width // tn,
                               q_scale=HEAD_DIM ** -0.5 * LOG2E)
    return pl.pallas_call(
        kernel,
        grid=(n // tn, t // tm),
        in_specs=[pl.BlockSpec((tm, d), lambda j, i: (i, 0)),
                  pl.BlockSpec((d, tn), lambda j, i: (0, j)),
                  pl.BlockSpec((tm, HEAD_DIM), lambda j, i: (i, 0)),
                  pl.BlockSpec((tm, HEAD_DIM), lambda j, i: (i, 0))],
        out_specs=pl.BlockSpec((tm, tn), lambda j, i: (i, j)),
        out_shape=jax.ShapeDtypeStruct((t, n), BF16),
        compiler_params=_params(("arbitrary", "arbitrary")),
        name="qk_rope",
    )(h, w_in, cos_t, sin_t)


def _proj_kernel(h_ref, w_ref, o_ref, *, sigmoid):
    for rows in _row_chunks(h_ref.shape[0]):
        h = h_ref[rows, :]
        for cols in _chunks(o_ref.shape[1]):
            acc = jnp.dot(h, w_ref[:, cols], preferred_element_type=F32)
            if sigmoid:
                acc = _sigmoid(acc)
            o_ref[rows, cols] = acc.astype(o_ref.dtype)


def _proj(h, w_in, col0, width, sigmoid, name):
    t, d = h.shape
    tm = _tile(t, 1024)
    tn = _tile(width, 1024)
    assert col0 % tn == 0
    j0 = col0 // tn
    return pl.pallas_call(
        functools.partial(_proj_kernel, sigmoid=sigmoid),
        grid=(width // tn, t // tm),
        in_specs=[pl.BlockSpec((tm, d), lambda j, i: (i, 0)),
                  pl.BlockSpec((d, tn), lambda j, i: (0, j0 + j))],
        out_specs=pl.BlockSpec((tm, tn), lambda j, i: (i, j)),
        out_shape=jax.ShapeDtypeStruct((t, width), BF16),
        compiler_params=_params(("arbitrary", "arbitrary")),
        name=name,
    )(h, w_in)


def _zero_history(hist_refs, first_of_seq):
    @pl.when(first_of_seq)
    def _():
        for ref in hist_refs:
            ref[:, 0:HALO, :] = jnp.zeros((ref.shape[0], HALO, ref.shape[2]), F32)


def _causal_conv3(u, w_ref, cols, hist_ref, c, rows=None):
    r0 = 0 if rows is None else rows.start
    n = u.shape[0]
    hist_ref[c, HALO + r0:HALO + r0 + n, :] = u
    u1 = hist_ref[c, HALO + r0 - 1:HALO + r0 - 1 + n, :]
    u2 = hist_ref[c, HALO + r0 - 2:HALO + r0 - 2 + n, :]
    return u * w_ref[2:3, cols] + u1 * w_ref[1:2, cols] + u2 * w_ref[0:1, cols]


def _carry_history(hist_refs, tm):
    for ref in hist_refs:
        ref[:, 0:HALO, :] = ref[:, tm:tm + HALO, :]


def _hist_scratch(tm, tn):
    chunks = _chunks(tn)
    return pltpu.VMEM((len(chunks), HALO + tm, tn // len(chunks)), F32)


def _convbranch_kernel(h_ref, wb_ref, wc_ref, wu_ref, cw_ref, o_ref, hist_ref, *, per_seq):
    _zero_history([hist_ref], pl.program_id(1) % per_seq == 0)
    h = h_ref[...]
    for c, cols in enumerate(_chunks(o_ref.shape[1])):
        cc = jnp.dot(h, wc_ref[:, cols], preferred_element_type=F32)
        cu = jnp.dot(h, wu_ref[:, cols], preferred_element_type=F32)
        y = _causal_conv3(cc * cu, cw_ref, cols, hist_ref, c)
        cb = jnp.dot(h, wb_ref[:, cols], preferred_element_type=F32)
        o_ref[:, cols] = (cb * y).astype(o_ref.dtype)
    _carry_history([hist_ref], h.shape[0])


def _convbranch(h, w_in, conv_w, col0, conv_width, seq):
    t, d = h.shape
    tm = _tile(seq, 1024)
    tn = _tile(conv_width, 512)
    nj = conv_width // tn
    assert col0 % tn == 0
    b0, c0, u0 = col0 // tn, (col0 + conv_width) // tn, (col0 + 2 * conv_width) // tn
    return pl.pallas_call(
        functools.partial(_convbranch_kernel, per_seq=seq // tm),
        grid=(nj, t // tm),
        in_specs=[pl.BlockSpec((tm, d), lambda j, i: (i, 0)),
                  pl.BlockSpec((d, tn), lambda j, i: (0, b0 + j)),
                  pl.BlockSpec((d, tn), lambda j, i: (0, c0 + j)),
                  pl.BlockSpec((d, tn), lambda j, i: (0, u0 + j)),
                  pl.BlockSpec((conv_w.shape[0], tn), lambda j, i: (0, j))],
        out_specs=pl.BlockSpec((tm, tn), lambda j, i: (i, j)),
        out_shape=jax.ShapeDtypeStruct((t, conv_width), BF16),
        scratch_shapes=[_hist_scratch(tm, tn)],
        compiler_params=_params(("arbitrary", "arbitrary")),
        name="conv_branch",
    )(h, w_in, w_in, w_in, conv_w)


KMEAN_TERMS = 3


def _skewed(n, stages):
    for step in range(n + len(stages) - 1):
        for s, stage in enumerate(stages):
            if 0 <= step - s < n:
                stage(step - s)


def _moba_kernel(q_ref, k_ref, v_ref, o_ref, kmean_ref, vt_ref, sel_ref, *, nb, hp, kc):
    i = pl.program_id(2)
    blk = MOBA_BLOCK
    span = kc * blk
    neg_inf = -jnp.inf
    heads = [slice(hh * HEAD_DIM, (hh + 1) * HEAD_DIM) for hh in range(hp)]
    nt_dims = (((1,), (1,)), ((), ()))

    @pl.when(i == 0)
    def _():
        for hh, hs in enumerate(heads):
            kf = k_ref[:, hs].astype(F32).reshape(nb, blk, HEAD_DIM)
            rest = jnp.sum(kf, axis=1) * (1.0 / blk)
            for term in range(KMEAN_TERMS):
                part = rest.astype(BF16)
                kmean_ref[hh, term * nb:(term + 1) * nb, :] = part
                rest = rest - part.astype(F32)
            for t in range(nb // kc):
                vt_ref[hh, t] = v_ref[t * span:(t + 1) * span, hs].T

    row = lax.broadcasted_iota(jnp.int32, (nb, blk), 0)
    q = [q_ref[:, hs] for hs in heads]
    for hh in range(hp):
        g3 = lax.dot_general(kmean_ref[hh], q[hh], nt_dims, preferred_element_type=F32)
        gate = g3[0:nb]
        for term in range(1, KMEAN_TERMS):
            gate = gate + g3[term * nb:(term + 1) * nb]
        gate = jnp.where(row < i, gate, neg_inf)
        sel = jnp.zeros((nb, blk), F32)
        for _ in range(MOBA_TOPK):
            best = jnp.max(gate, axis=0, keepdims=True)
            hit = (gate == best) & (best > neg_inf)
            first = jnp.min(jnp.where(hit, row, nb), axis=0, keepdims=True)
            pick = row == first
            sel = jnp.where(pick, 1.0, sel)
            gate = jnp.where(pick, neg_inf, gate)
        for n in range(nb):
            sel_ref[hh, n] = sel[n:n + 1, :]

    def scores(hh, start, size):
        kj = k_ref[pl.ds(start, size), heads[hh]]
        return lax.dot_general(kj, q[hh], nt_dims, preferred_element_type=F32)

    own = pl.multiple_of(i * blk, blk)
    kpos = lax.broadcasted_iota(jnp.int32, (blk, blk), 0)
    qpos = lax.broadcasted_iota(jnp.int32, (blk, blk), 1)
    causal = kpos <= qpos
    s_own, p_own, carry = [None] * hp, [None] * hp, [None] * hp

    def own_scores(hh):
        s_own[hh] = jnp.where(causal, scores(hh, own, blk), neg_inf)

    def own_softmax(hh):
        m = jnp.max(s_own[hh], axis=0, keepdims=True)
        p_own[hh] = jnp.exp2(s_own[hh] - m)
        carry[hh] = (m, jnp.sum(p_own[hh], axis=0, keepdims=True))

    def own_pv(hh):
        vj = v_ref[pl.ds(own, blk), heads[hh]]
        acc = lax.dot_general(vj, p_own[hh].astype(vj.dtype), (((0,), (0,)), ((), ())),
                              preferred_element_type=F32)
        carry[hh] = carry[hh] + (acc,)

    _skewed(hp, [own_scores, own_softmax, own_pv])

    def body(t, carry):
        start = pl.multiple_of(t * span, span)
        s_all, p_all, stat, out = [None] * hp, [None] * hp, [None] * hp, [None] * hp

        def past_scores(hh):
            chosen = sel_ref[hh, pl.ds(t * kc, kc)]
            chosen = jnp.broadcast_to(chosen, (kc, blk, blk)).reshape(span, blk)
            s_all[hh] = jnp.where(chosen > 0.0, scores(hh, start, span), neg_inf)

        def past_softmax(hh):
            m, l, _ = carry[hh]
            m_new = jnp.maximum(m, jnp.max(s_all[hh], axis=0, keepdims=True))
            alpha = jnp.exp2(m - m_new)
            p = jnp.exp2(s_all[hh] - m_new)
            p_all[hh] = p.astype(BF16)
            stat[hh] = (m_new, alpha * l + jnp.sum(p, axis=0, keepdims=True), alpha)

        def past_pv(hh):
            m_new, l, alpha = stat[hh]
            acc = alpha * carry[hh][2] + jnp.dot(vt_ref[hh, t], p_all[hh],
                                                 preferred_element_type=F32)
            out[hh] = (m_new, l, acc)

        _skewed(hp, [past_scores, past_softmax, past_pv])
        return tuple(out)

    carry = lax.fori_loop(0, (i + kc - 1) // kc, body, tuple(carry))
    for hh in range(hp):
        _, l, acc = carry[hh]
        o_ref[:, heads[hh]] = (acc / l).T.astype(o_ref.dtype)


def _moba(qk, v, batch, seq, n_heads):
    t = qk.shape[0]
    nb = seq // MOBA_BLOCK
    hp = 4 if n_heads % 4 == 0 else (2 if n_heads % 2 == 0 else 1)
    kc = 2 if nb % 2 == 0 else 1
    ng = n_heads // hp
    w = hp * HEAD_DIM
    return pl.pallas_call(
        functools.partial(_moba_kernel, nb=nb, hp=hp, kc=kc),
        grid=(batch, ng, nb),
        in_specs=[pl.BlockSpec((MOBA_BLOCK, w), lambda b, g, i: (b * nb + i, g)),
                  pl.BlockSpec((seq, w), lambda b, g, i: (b, ng + g)),
                  pl.BlockSpec((seq, w), lambda b, g, i: (b, g))],
        out_specs=pl.BlockSpec((MOBA_BLOCK, w), lambda b, g, i: (b * nb + i, g)),
        out_shape=jax.ShapeDtypeStruct((t, n_heads * HEAD_DIM), BF16),
        scratch_shapes=[pltpu.VMEM((hp, KMEAN_TERMS * nb, HEAD_DIM), BF16),
                        pltpu.VMEM((hp, nb // kc, HEAD_DIM, kc * MOBA_BLOCK), BF16),
                        pltpu.VMEM((hp, nb, 1, MOBA_BLOCK), F32)],
        compiler_params=_params(("arbitrary", "arbitrary", "arbitrary")),
        name="moba",
    )(qk, qk, v)


def _merge_kernel(a_ref, z_ref, wa_ref, wz_ref, ga_ref, gz_ref, o_ref):
    a = a_ref[...]
    z = z_ref[...]
    for cols in _chunks(o_ref.shape[1]):
        ya = jnp.dot(a, wa_ref[:, cols], preferred_element_type=F32)
        yz = jnp.dot(z, wz_ref[:, cols], preferred_element_type=F32)
        o_ref[:, cols] = (ga_ref[:, cols].astype(F32) * ya
                          + gz_ref[:, cols].astype(F32) * yz).astype(o_ref.dtype)


def _merge(attn, z, w_attn_out, w_conv_out, gates):
    t, ka = attn.shape
    kz = z.shape[1]
    n = w_attn_out.shape[1]
    tm = _tile(t, 1024)
    tn = _tile(n, 1024)
    nj = n // tn
    return pl.pallas_call(
        _merge_kernel,
        grid=(nj, t // tm),
        in_specs=[pl.BlockSpec((tm, ka), lambda j, i: (i, 0)),
                  pl.BlockSpec((tm, kz), lambda j, i: (i, 0)),
                  pl.BlockSpec((ka, tn), lambda j, i: (0, j)),
                  pl.BlockSpec((kz, tn), lambda j, i: (0, j)),
                  pl.BlockSpec((tm, tn), lambda j, i: (i, j)),
                  pl.BlockSpec((tm, tn), lambda j, i: (i, nj + j))],
        out_specs=pl.BlockSpec((tm, tn), lambda j, i: (i, j)),
        out_shape=jax.ShapeDtypeStruct((t, n), BF16),
        compiler_params=_params(("arbitrary", "arbitrary")),
        name="merge",
    )(attn, z, w_attn_out, w_conv_out, gates, gates)


def _resid_full_kernel(a_ref, w_ref, x_ref, g_ref, o_ref):
    a = a_ref[...]
    for cols in _chunks(o_ref.shape[1]):
        y = jnp.dot(a, w_ref[:, cols], preferred_element_type=F32)
        o_ref[:, cols] = x_ref[:, cols] + g_ref[0, :, cols] * y


def _resid_full(a, w, x2d, mod3, gate_idx, seq):
    t, kdim = a.shape
    n = w.shape[1]
    tm = _tile(seq, 1024)
    tn = _tile(n, 512)
    per = seq // tm
    return pl.pallas_call(
        _resid_full_kernel,
        grid=(n // tn, t // tm),
        in_specs=[pl.BlockSpec((tm, kdim), lambda j, i: (i, 0)),
                  pl.BlockSpec((kdim, tn), lambda j, i: (0, j)),
                  pl.BlockSpec((tm, tn), lambda j, i: (i, j)),
                  pl.BlockSpec((1, 1, tn), lambda j, i: ((i // per) * N_MOD + gate_idx, 0, j))],
        out_specs=pl.BlockSpec((tm, tn), lambda j, i: (i, j)),
        out_shape=jax.ShapeDtypeStruct((t, n), F32),
        compiler_params=_params(("arbitrary", "arbitrary")),
        name="resid_full",
    )(a, w, x2d, mod3)


def _resid_ktiled_kernel(a_ref, w_ref, x_ref, g_ref, o_ref, acc_ref):
    k = pl.program_id(2)

    @pl.when((pl.program_id(0) == 0) & (pl.program_id(1) == 0) & (k == 0))
    def _():
        acc_ref[...] = jnp.zeros(acc_ref.shape, F32)

    for rows in _row_chunks(a_ref.shape[0]):
        a = a_ref[rows, :]
        for cols in _chunks(o_ref.shape[1]):
            part = jnp.dot(a, w_ref[:, cols], preferred_element_type=F32)
            acc = jnp.where(k == 0, 0.0, acc_ref[rows, cols]) + part
            acc_ref[rows, cols] = acc
            o_ref[rows, cols] = x_ref[rows, cols] + g_ref[0, :, cols] * acc


def _resid_ktiled(a, w, x2d, mod3, gate_idx, seq, tk_target):
    t, kdim = a.shape
    n = w.shape[1]
    tm = _tile(seq, 1024)
    tn = _tile(n, 1024)
    tk = _tile(kdim, tk_target)
    per = seq // tm
    return pl.pallas_call(
        _resid_ktiled_kernel,
        grid=(n // tn, t // tm, kdim // tk),
        in_specs=[pl.BlockSpec((tm, tk), lambda j, i, k: (i, k)),
                  pl.BlockSpec((tk, tn), lambda j, i, k: (k, j)),
                  pl.BlockSpec((tm, tn), lambda j, i, k: (i, j)),
                  pl.BlockSpec((1, 1, tn), lambda j, i, k: ((i // per) * N_MOD + gate_idx, 0, j))],
        out_specs=pl.BlockSpec((tm, tn), lambda j, i, k: (i, j)),
        out_shape=jax.ShapeDtypeStruct((t, n), F32),
        scratch_shapes=[pltpu.VMEM((tm, tn), F32)],
        compiler_params=_params(("arbitrary", "arbitrary", "arbitrary")),
        name="resid_ktiled",
    )(a, w, x2d, mod3)


def _ffn_up_kernel(h_ref, wg_ref, wv_ref, cg_ref, cv_ref, o_ref, hg_ref, hv_ref, *, per_seq):
    _zero_history([hg_ref, hv_ref], pl.program_id(1) % per_seq == 0)
    tm = h_ref.shape[0]
    for rows in _row_chunks(tm):
        h = h_ref[rows, :]
        for c, cols in enumerate(_chunks(o_ref.shape[1])):
            ug = _causal_conv3(jnp.dot(h, wg_ref[:, cols], preferred_element_type=F32),
                               cg_ref, cols, hg_ref, c, rows)
            uv = _causal_conv3(jnp.dot(h, wv_ref[:, cols], preferred_element_type=F32),
                               cv_ref, cols, hv_ref, c, rows)
            o_ref[rows, cols] = (ug * _sigmoid(ug) * uv).astype(o_ref.dtype)
    _carry_history([hg_ref, hv_ref], tm)


def _ffn_up(h2, w_up, conv_w, d_ff, seq):
    t, d = h2.shape
    tm = _tile(seq, 1024)
    tn = _tile(d_ff, 512)
    nj = d_ff // tn
    kw = conv_w.shape[0]
    return pl.pallas_call(
        functools.partial(_ffn_up_kernel, per_seq=seq // tm),
        grid=(nj, t // tm),
        in_specs=[pl.BlockSpec((tm, d), lambda j, i: (i, 0)),
                  pl.BlockSpec((d, tn), lambda j, i: (0, j)),
                  pl.BlockSpec((d, tn), lambda j, i: (0, nj + j)),
                  pl.BlockSpec((kw, tn), lambda j, i: (0, j)),
                  pl.BlockSpec((kw, tn), lambda j, i: (0, nj + j))],
        out_specs=pl.BlockSpec((tm, tn), lambda j, i: (i, j)),
        out_shape=jax.ShapeDtypeStruct((t, d_ff), BF16),
        scratch_shapes=[_hist_scratch(tm, tn), _hist_scratch(tm, tn)],
        compiler_params=_params(("arbitrary", "arbitrary")),
        name="ffn_up",
    )(h2, w_up, w_up, conv_w, conv_w)


def kernel(x, c, positions, w_ada, b_ada, g_mix, w_in, conv_w, w_attn_out, w_conv_out,
           w_o, g_ffn, w_up, ffn_conv_w, w_down, g_final):
    b, s, d = x.shape
    depth = w_ada.shape[0]
    a = w_attn_out.shape[1]
    cw = w_conv_out.shape[1]
    d_ff = w_down.shape[1]
    n_heads = a // HEAD_DIM
    assert conv_w.shape[1] == 3 and ffn_conv_w.shape[1] == 3
    assert s % MOBA_BLOCK == 0 and w_in.shape[2] == 3 * a + 3 * cw + 2 * d

    xt = x.reshape(b * s, d)
    cos_t, sin_t = _rope_tables(positions)
    for layer in range(depth):
        mod3 = _ada(c, w_ada[layer], b_ada[layer]).reshape(b * N_MOD, 1, d)
        w_in_l = w_in[layer].astype(BF16)

        h = _normmod(xt, g_mix[layer], mod3, 0, 1, s)
        qk = _qk(h, w_in_l, cos_t, sin_t, a)
        v = _proj(h, w_in_l, 2 * a, a, False, "v_proj")
        z = _convbranch(h, w_in_l, conv_w[layer], 3 * a, cw, s)
        gates = _proj(h, w_in_l, 3 * a + 3 * cw, 2 * d, True, "gates")
        attn = _moba(qk, v, b, s, n_heads)
        merged = _merge(attn, z, w_attn_out[layer].astype(BF16), w_conv_out[layer].astype(BF16), gates)
        xt = _resid_full(merged, w_o[layer].astype(BF16), xt, mod3, 2, s)

        h2 = _normmod(xt, g_ffn[layer], mod3, 3, 4, s)
        act = _ffn_up(h2, w_up[layer].astype(BF16), ffn_conv_w[layer], d_ff, s)
        xt = _resid_ktiled(act, w_down[layer].astype(BF16), xt, mod3, 5, s, 2048)
    return _final_norm(xt, g_final).reshape(b, s, d)
```

```python
import functools
import math

import jax
import jax.numpy as jnp
from jax import lax
from jax.experimental import pallas as pl
from jax.experimental.pallas import tpu as pltpu

F32 = jnp.float32
BF16 = jnp.bfloat16

HEAD_DIM = 128
ROT_DIM = HEAD_DIM // 4
ROPE_THETA = 500000.0
MOBA_BLOCK = 256
MOBA_TOPK = 3
EPS = 1e-6
N_MOD = 6
LANES = 128
MXU_COLS = 256
ROW_CHUNK = 256
HALO = 8
VMEM_LIMIT = 56 * 1024 * 1024
LOG2E = math.log2(math.e)


def _tile(dim, target):
    if dim <= target:
        return dim
    t = (target // LANES) * LANES
    while t >= LANES:
        if dim % t == 0:
            return t
        t -= LANES
    return dim


def _chunks(tn):
    w = MXU_COLS if tn % MXU_COLS == 0 else tn
    return [slice(c * w, (c + 1) * w) for c in range(tn // w)]


def _row_chunks(tm):
    r = ROW_CHUNK if tm % ROW_CHUNK == 0 else tm
    return [slice(k * r, (k + 1) * r) for k in range(tm // r)]


def _params(sem):
    return pltpu.CompilerParams(dimension_semantics=sem, vmem_limit_bytes=VMEM_LIMIT)


def _sigmoid(x):
    return 1.0 / (1.0 + jnp.exp2(x * (-LOG2E)))


def _ada_kernel(c_ref, w_ref, b_ref, o_ref):
    c = c_ref[...]
    c_act = c * _sigmoid(c)
    acc = jnp.dot(c_act.astype(BF16), w_ref[...].astype(BF16), preferred_element_type=F32)
    o_ref[...] = acc + b_ref[...]


def _ada(c, w, bias):
    b, d = c.shape
    n = w.shape[1]
    tn = _tile(n, 512)
    return pl.pallas_call(
        _ada_kernel,
        grid=(n // tn,),
        in_specs=[pl.BlockSpec((b, d), lambda j: (0, 0)),
                  pl.BlockSpec((d, tn), lambda j: (0, j)),
                  pl.BlockSpec((1, tn), lambda j: (0, j))],
        out_specs=pl.BlockSpec((b, tn), lambda j: (0, j)),
        out_shape=jax.ShapeDtypeStruct((b, n), F32),
        compiler_params=_params(("arbitrary",)),
        name="ada",
    )(c, w, bias.reshape(1, n))


def _rope_kernel(pos_ref, invf_ref, sign_ref, cos_ref, sin_ref):
    ang = pos_ref[...].astype(F32) * invf_ref[...]
    cos_ref[...] = jnp.cos(ang)
    sin_ref[...] = jnp.sin(ang) * sign_ref[...]


def _rope_tables(positions):
    t = positions.size
    half = ROT_DIM // 2
    inv_freq = ROPE_THETA ** (-jnp.arange(0, ROT_DIM, 2, dtype=F32) / ROT_DIM)
    invf = jnp.zeros((HEAD_DIM,), F32).at[:ROT_DIM].set(jnp.tile(inv_freq, 2)).reshape(1, HEAD_DIM)
    sign = jnp.ones((HEAD_DIM,), F32).at[:half].set(-1.0).reshape(1, HEAD_DIM)
    tm = _tile(t, 2048)
    return pl.pallas_call(
        _rope_kernel,
        grid=(t // tm,),
        in_specs=[pl.BlockSpec((tm, 1), lambda i: (i, 0)),
                  pl.BlockSpec((1, HEAD_DIM), lambda i: (0, 0)),
                  pl.BlockSpec((1, HEAD_DIM), lambda i: (0, 0))],
        out_specs=[pl.BlockSpec((tm, HEAD_DIM), lambda i: (i, 0)),
                   pl.BlockSpec((tm, HEAD_DIM), lambda i: (i, 0))],
        out_shape=[jax.ShapeDtypeStruct((t, HEAD_DIM), F32)] * 2,
        compiler_params=_params(("arbitrary",)),
        name="rope_tables",
    )(positions.reshape(t, 1), invf, sign)


def _normmod_kernel(x_ref, g_ref, shift_ref, scale_ref, o_ref):
    x = x_ref[...]
    y = x * lax.rsqrt(jnp.mean(x * x, axis=-1, keepdims=True) + EPS) * g_ref[...]
    o_ref[...] = (y * (1.0 + scale_ref[0]) + shift_ref[0]).astype(o_ref.dtype)


def _normmod(x2d, g, mod3, shift_idx, scale_idx, seq):
    t, d = x2d.shape
    tm = _tile(seq, 256)
    per = seq // tm
    return pl.pallas_call(
        _normmod_kernel,
        grid=(t // tm,),
        in_specs=[pl.BlockSpec((tm, d), lambda i: (i, 0)),
                  pl.BlockSpec((1, d), lambda i: (0, 0)),
                  pl.BlockSpec((1, 1, d), lambda i: ((i // per) * N_MOD + shift_idx, 0, 0)),
                  pl.BlockSpec((1, 1, d), lambda i: ((i // per) * N_MOD + scale_idx, 0, 0))],
        out_specs=pl.BlockSpec((tm, d), lambda i: (i, 0)),
        out_shape=jax.ShapeDtypeStruct((t, d), BF16),
        compiler_params=_params(("arbitrary",)),
        name="normmod",
    )(x2d, g.reshape(1, d), mod3, mod3)


def _final_norm_kernel(x_ref, g_ref, o_ref):
    x = x_ref[...]
    o_ref[...] = x * lax.rsqrt(jnp.mean(x * x, axis=-1, keepdims=True) + EPS) * g_ref[...]


def _final_norm(x2d, g):
    t, d = x2d.shape
    tm = _tile(t, 256)
    return pl.pallas_call(
        _final_norm_kernel,
        grid=(t // tm,),
        in_specs=[pl.BlockSpec((tm, d), lambda i: (i, 0)),
                  pl.BlockSpec((1, d), lambda i: (0, 0))],
        out_specs=pl.BlockSpec((tm, d), lambda i: (i, 0)),
        out_shape=jax.ShapeDtypeStruct((t, d), F32),
        compiler_params=_params(("arbitrary",)),
        name="final_norm",
    )(x2d, g.reshape(1, d))


def _qk_kernel(h_ref, w_ref, cos_ref, sin_ref, o_ref, *, q_tiles, q_scale):
    fac = jnp.where(pl.program_id(0) < q_tiles, q_scale, 1.0).astype(F32)
    half = ROT_DIM // 2
    for rows in _row_chunks(h_ref.shape[0]):
        h = h_ref[rows, :]
        cos = cos_ref[rows, :] * fac
        sin = sin_ref[rows, :] * fac
        for cols in _chunks(o_ref.shape[1]):
            acc = jnp.dot(h, w_ref[:, cols], preferred_element_type=F32)
            wc = acc.shape[1]
            reps = wc // HEAD_DIM
            lane = lax.broadcasted_iota(jnp.int32, acc.shape, 1) % HEAD_DIM
            partner = jnp.where(lane < half, pltpu.roll(acc, wc - half, 1),
                                pltpu.roll(acc, half, 1))
            o_ref[rows, cols] = (acc * jnp.tile(cos, (1, reps))
                                 + partner * jnp.tile(sin, (1, reps))).astype(o_ref.dtype)


def _qk(h, w_in, cos_t, sin_t, attn_width):
    t, d = h.shape
    n = 2 * attn_width
    tm = _tile(t, 1024)
    tn = _tile(attn_width, 1024)
    kernel = functools.partial(_qk_kernel, q_tiles=attn_width // tn,
                               q_scale=HEAD_DIM ** -0.5 * LOG2E)
    return pl.pallas_call(
        kernel,
        grid=(n // tn, t // tm),
        in_specs=[pl.BlockSpec((tm, d), lambda j, i: (i, 0)),
                  pl.BlockSpec((d, tn), lambda j, i: (0, j)),
                  pl.BlockSpec((tm, HEAD_DIM), lambda j, i: (i, 0)),
                  pl.BlockSpec((tm, HEAD_DIM), lambda j, i: (i, 0))],
        out_specs=pl.BlockSpec((tm, tn), lambda j, i: (i, j)),
        out_shape=jax.ShapeDtypeStruct((t, n), BF16),
        compiler_params=_params(("arbitrary", "arbitrary")),
        name="qk_rope",
    )(h, w_in, cos_t, sin_t)


def _proj_kernel(h_ref, w_ref, o_ref, *, sigmoid):
    for rows in _row_chunks(h_ref.shape[0]):
        h = h_ref[rows, :]
        for cols in _chunks(o_ref.shape[1]):
            acc = jnp.dot(h, w_ref[:, cols], preferred_element_type=F32)
            if sigmoid:
                acc = _sigmoid(acc)
            o_ref[rows, cols] = acc.astype(o_ref.dtype)


def _proj(h, w_in, col0, width, sigmoid, name):
    t, d = h.shape
    tm = _tile(t, 1024)
    tn = _tile(width, 1024)
    assert col0 % tn == 0
    j0 = col0 // tn
    return pl.pallas_call(
        functools.partial(_proj_kernel, sigmoid=sigmoid),
        grid=(width // tn, t // tm),
        in_specs=[pl.BlockSpec((tm, d), lambda j, i: (i, 0)),
                  pl.BlockSpec((d, tn), lambda j, i: (0, j0 + j))],
        out_specs=pl.BlockSpec((tm, tn), lambda j, i: (i, j)),
        out_shape=jax.ShapeDtypeStruct((t, width), BF16),
        compiler_params=_params(("arbitrary", "arbitrary")),
        name=name,
    )(h, w_in)


def _zero_history(hist_refs, first_of_seq):
    @pl.when(first_of_seq)
    def _():
        for ref in hist_refs:
            ref[:, 0:HALO, :] = jnp.zeros((ref.shape[0], HALO, ref.shape[2]), F32)


def _causal_conv3(u, w_ref, cols, hist_ref, c, rows=None):
    r0 = 0 if rows is None else rows.start
    n = u.shape[0]
    hist_ref[c, HALO + r0:HALO + r0 + n, :] = u
    u1 = hist_ref[c, HALO + r0 - 1:HALO + r0 - 1 + n, :]
    u2 = hist_ref[c, HALO + r0 - 2:HALO + r0 - 2 + n, :]
    return u * w_ref[2:3, cols] + u1 * w_ref[1:2, cols] + u2 * w_ref[0:1, cols]


def _carry_history(hist_refs, tm):
    for ref in hist_refs:
        ref[:, 0:HALO, :] = ref[:, tm:tm + HALO, :]


def _hist_scratch(tm, tn):
    chunks = _chunks(tn)
    return pltpu.VMEM((len(chunks), HALO + tm, tn // len(chunks)), F32)


def _convbranch_kernel(h_ref, wb_ref, wc_ref, wu_ref, cw_ref, o_ref, hist_ref, *, per_seq):
    _zero_history([hist_ref], pl.program_id(1) % per_seq == 0)
    h = h_ref[...]
    for c, cols in enumerate(_chunks(o_ref.shape[1])):
        cc = jnp.dot(h, wc_ref[:, cols], preferred_element_type=F32)
        cu = jnp.dot(h, wu_ref[:, cols], preferred_element_type=F32)
        y = _causal_conv3(cc * cu, cw_ref, cols, hist_ref, c)
        cb = jnp.dot(h, wb_ref[:, cols], preferred_element_type=F32)
        o_ref[:, cols] = (cb * y).astype(o_ref.dtype)
    _carry_history([hist_ref], h.shape[0])


def _convbranch(h, w_in, conv_w, col0, conv_width, seq):
    t, d = h.shape
    tm = _tile(seq, 1024)
    tn = _tile(conv_width, 512)
    nj = conv_width // tn
    assert col0 % tn == 0
    b0, c0, u0 = col0 // tn, (col0 + conv_width) // tn, (col0 + 2 * conv_width) // tn
    return pl.pallas_call(
        functools.partial(_convbranch_kernel, per_seq=seq // tm),
        grid=(nj, t // tm),
        in_specs=[pl.BlockSpec((tm, d), lambda j, i: (i, 0)),
                  pl.BlockSpec((d, tn), lambda j, i: (0, b0 + j)),
                  pl.BlockSpec((d, tn), lambda j, i: (0, c0 + j)),
                  pl.BlockSpec((d, tn), lambda j, i: (0, u0 + j)),
                  pl.BlockSpec((conv_w.shape[0], tn), lambda j, i: (0, j))],
        out_specs=pl.BlockSpec((tm, tn), lambda j, i: (i, j)),
        out_shape=jax.ShapeDtypeStruct((t, conv_width), BF16),
        scratch_shapes=[_hist_scratch(tm, tn)],
        compiler_params=_params(("arbitrary", "arbitrary")),
        name="conv_branch",
    )(h, w_in, w_in, w_in, conv_w)


KMEAN_TERMS = 3


def _skewed(n, stages):
    for step in range(n + len(stages) - 1):
        for s, stage in enumerate(stages):
            if 0 <= step - s < n:
                stage(step - s)


def _moba_kernel(q_ref, k_ref, v_ref, o_ref, kmean_ref, vt_ref, sel_ref, *, nb, hp, kc):
    i = pl.program_id(2)
    blk = MOBA_BLOCK
    span = kc * blk
    neg_inf = -jnp.inf
    heads = [slice(hh * HEAD_DIM, (hh + 1) * HEAD_DIM) for hh in range(hp)]
    nt_dims = (((1,), (1,)), ((), ()))

    @pl.when(i == 0)
    def _():
        for hh, hs in enumerate(heads):
            kf = k_ref[:, hs].astype(F32).reshape(nb, blk, HEAD_DIM)
            rest = jnp.sum(kf, axis=1) * (1.0 / blk)
            for term in range(KMEAN_TERMS):
                part = rest.astype(BF16)
                kmean_ref[hh, term * nb:(term + 1) * nb, :] = part
                rest = rest - part.astype(F32)
            for t in range(nb // kc):
                vt_ref[hh, t] = v_ref[t * span:(t + 1) * span, hs].T

    row = lax.broadcasted_iota(jnp.int32, (nb, blk), 0)
    q = [q_ref[:, hs] for hs in heads]
    for hh in range(hp):
        g3 = lax.dot_general(kmean_ref[hh], q[hh], nt_dims, preferred_element_type=F32)
        gate = g3[0:nb]
        for term in range(1, KMEAN_TERMS):
            gate = gate + g3[term * nb:(term + 1) * nb]
        gate = jnp.where(row < i, gate, neg_inf)
        sel = jnp.zeros((nb, blk), F32)
        for _ in range(MOBA_TOPK):
            best = jnp.max(gate, axis=0, keepdims=True)
            hit = (gate == best) & (best > neg_inf)
            first = jnp.min(jnp.where(hit, row, nb), axis=0, keepdims=True)
            pick = row == first
            sel = jnp.where(pick, 1.0, sel)
            gate = jnp.where(pick, neg_inf, gate)
        for n in range(nb):
            sel_ref[hh, n] = sel[n:n + 1, :]

    def scores(hh, start, size):
        kj = k_ref[pl.ds(start, size), heads[hh]]
        return lax.dot_general(kj, q[hh], nt_dims, preferred_element_type=F32)

    own = pl.multiple_of(i * blk, blk)
    kpos = lax.broadcasted_iota(jnp.int32, (blk, blk), 0)
    qpos = lax.broadcasted_iota(jnp.int32, (blk, blk), 1)
    causal = kpos <= qpos
    s_own, p_own, carry = [None] * hp, [None] * hp, [None] * hp

    def own_scores(hh):
        s_own[hh] = jnp.where(causal, scores(hh, own, blk), neg_inf)

    def own_softmax(hh):
        m = jnp.max(s_own[hh], axis=0, keepdims=True)
        p_own[hh] = jnp.exp2(s_own[hh] - m)
        carry[hh] = (m, jnp.sum(p_own[hh], axis=0, keepdims=True))

    def own_pv(hh):
        vj = v_ref[pl.ds(own, blk), heads[hh]]
        acc = lax.dot_general(vj, p_own[hh].astype(vj.dtype), (((0,), (0,)), ((), ())),
                              preferred_element_type=F32)
        carry[hh] = carry[hh] + (acc,)

    _skewed(hp, [own_scores, own_softmax, own_pv])

    def body(t, carry):
        start = pl.multiple_of(t * span, span)
        s_all, p_all, stat, out = [None] * hp, [None] * hp, [None] * hp, [None] * hp

        def past_scores(hh):
            chosen = sel_ref[hh, pl.ds(t * kc, kc)]
            chosen = jnp.broadcast_to(chosen, (kc, blk, blk)).reshape(span, blk)
            s_all[hh] = jnp.where(chosen > 0.0, scores(hh, start, span), neg_inf)

        def past_softmax(hh):
            m, l, _ = carry[hh]
            m_new = jnp.maximum(m, jnp.max(s_all[hh], axis=0, keepdims=True))
            alpha = jnp.exp2(m - m_new)
            p = jnp.exp2(s_all[hh] - m_new)
            p_all[hh] = p.astype(BF16)
            stat[hh] = (m_new, alpha * l + jnp.sum(p, axis=0, keepdims=True), alpha)

        def past_pv(hh):
            m_new, l, alpha = stat[hh]
            acc = alpha * carry[hh][2] + jnp.dot(vt_ref[hh, t], p_all[hh],
                                                 preferred_element_type=F32)
            out[hh] = (m_new, l, acc)

        _skewed(hp, [past_scores, past_softmax, past_pv])
        return tuple(out)

    carry = lax.fori_loop(0, (i + kc - 1) // kc, body, tuple(carry))
    for hh in range(hp):
        _, l, acc = carry[hh]
        o_ref[:, heads[hh]] = (acc / l).T.astype(o_ref.dtype)


def _moba(qk, v, batch, seq, n_heads):
    t = qk.shape[0]
    nb = seq // MOBA_BLOCK
    hp = next(n for n in (8, 4, 2, 1) if n_heads % n == 0)
    kc = 2 if nb % 2 == 0 else 1
    ng = n_heads // hp
    w = hp * HEAD_DIM
    return pl.pallas_call(
        functools.partial(_moba_kernel, nb=nb, hp=hp, kc=kc),
        grid=(batch, ng, nb),
        in_specs=[pl.BlockSpec((MOBA_BLOCK, w), lambda b, g, i: (b * nb + i, g)),
                  pl.BlockSpec((seq, w), lambda b, g, i: (b, ng + g)),
                  pl.BlockSpec((seq, w), lambda b, g, i: (b, g))],
        out_specs=pl.BlockSpec((MOBA_BLOCK, w), lambda b, g, i: (b * nb + i, g)),
        out_shape=jax.ShapeDtypeStruct((t, n_heads * HEAD_DIM), BF16),
        scratch_shapes=[pltpu.VMEM((hp, KMEAN_TERMS * nb, HEAD_DIM), BF16),
                        pltpu.VMEM((hp, nb // kc, HEAD_DIM, kc * MOBA_BLOCK), BF16),
                        pltpu.VMEM((hp, nb, 1, MOBA_BLOCK), F32)],
        compiler_params=_params(("arbitrary", "arbitrary", "arbitrary")),
        name="moba",
    )(qk, qk, v)


def _merge_kernel(a_ref, z_ref, wa_ref, wz_ref, ga_ref, gz_ref, o_ref):
    a = a_ref[...]
    z = z_ref[...]
    for cols in _chunks(o_ref.shape[1]):
        ya = jnp.dot(a, wa_ref[:, cols], preferred_element_type=F32)
        yz = jnp.dot(z, wz_ref[:, cols], preferred_element_type=F32)
        o_ref[:, cols] = (ga_ref[:, cols].astype(F32) * ya
                          + gz_ref[:, cols].astype(F32) * yz).astype(o_ref.dtype)


def _merge(attn, z, w_attn_out, w_conv_out, gates):
    t, ka = attn.shape
    kz = z.shape[1]
    n = w_attn_out.shape[1]
    tm = _tile(t, 1024)
    tn = _tile(n, 1024)
    nj = n // tn
    return pl.pallas_call(
        _merge_kernel,
        grid=(nj, t // tm),
        in_specs=[pl.BlockSpec((tm, ka), lambda j, i: (i, 0)),
                  pl.BlockSpec((tm, kz), lambda j, i: (i, 0)),
                  pl.BlockSpec((ka, tn), lambda j, i: (0, j)),
                  pl.BlockSpec((kz, tn), lambda j, i: (0, j)),
                  pl.BlockSpec((tm, tn), lambda j, i: (i, j)),
                  pl.BlockSpec((tm, tn), lambda j, i: (i, nj + j))],
        out_specs=pl.BlockSpec((tm, tn), lambda j, i: (i, j)),
        out_shape=jax.ShapeDtypeStruct((t, n), BF16),
        compiler_params=_params(("arbitrary", "arbitrary")),
        name="merge",
    )(attn, z, w_attn_out, w_conv_out, gates, gates)


def _resid_full_kernel(a_ref, w_ref, x_ref, g_ref, o_ref):
    a = a_ref[...]
    for cols in _chunks(o_ref.shape[1]):
        y = jnp.dot(a, w_ref[:, cols], preferred_element_type=F32)
        o_ref[:, cols] = x_ref[:, cols] + g_ref[0, :, cols] * y


def _resid_full(a, w, x2d, mod3, gate_idx, seq):
    t, kdim = a.shape
    n = w.shape[1]
    tm = _tile(seq, 1024)
    tn = _tile(n, 512)
    per = seq // tm
    return pl.pallas_call(
        _resid_full_kernel,
        grid=(n // tn, t // tm),
        in_specs=[pl.BlockSpec((tm, kdim), lambda j, i: (i, 0)),
                  pl.BlockSpec((kdim, tn), lambda j, i: (0, j)),
                  pl.BlockSpec((tm, tn), lambda j, i: (i, j)),
                  pl.BlockSpec((1, 1, tn), lambda j, i: ((i // per) * N_MOD + gate_idx, 0, j))],
        out_specs=pl.BlockSpec((tm, tn), lambda j, i: (i, j)),
        out_shape=jax.ShapeDtypeStruct((t, n), F32),
        compiler_params=_params(("arbitrary", "arbitrary")),
        name="resid_full",
    )(a, w, x2d, mod3)


def _resid_ktiled_kernel(a_ref, w_ref, x_ref, g_ref, o_ref, acc_ref):
    k = pl.program_id(2)

    @pl.when((pl.program_id(0) == 0) & (pl.program_id(1) == 0) & (k == 0))
    def _():
        acc_ref[...] = jnp.zeros(acc_ref.shape, F32)

    for rows in _row_chunks(a_ref.shape[0]):
        a = a_ref[rows, :]
        for cols in _chunks(o_ref.shape[1]):
            part = jnp.dot(a, w_ref[:, cols], preferred_element_type=F32)
            acc = jnp.where(k == 0, 0.0, acc_ref[rows, cols]) + part
            acc_ref[rows, cols] = acc
            o_ref[rows, cols] = x_ref[rows, cols] + g_ref[0, :, cols] * acc


def _resid_ktiled(a, w, x2d, mod3, gate_idx, seq, tk_target):
    t, kdim = a.shape
    n = w.shape[1]
    tm = _tile(seq, 1024)
    tn = _tile(n, 1024)
    tk = _tile(kdim, tk_target)
    per = seq // tm
    return pl.pallas_call(
        _resid_ktiled_kernel,
        grid=(n // tn, t // tm, kdim // tk),
        in_specs=[pl.BlockSpec((tm, tk), lambda j, i, k: (i, k)),
                  pl.BlockSpec((tk, tn), lambda j, i, k: (k, j)),
                  pl.BlockSpec((tm, tn), lambda j, i, k: (i, j)),
                  pl.BlockSpec((1, 1, tn), lambda j, i, k: ((i // per) * N_MOD + gate_idx, 0, j))],
        out_specs=pl.BlockSpec((tm, tn), lambda j, i, k: (i, j)),
        out_shape=jax.ShapeDtypeStruct((t, n), F32),
        scratch_shapes=[pltpu.VMEM((tm, tn), F32)],
        compiler_params=_params(("arbitrary", "arbitrary", "arbitrary")),
        name="resid_ktiled",
    )(a, w, x2d, mod3)


def _ffn_up_kernel(h_ref, wg_ref, wv_ref, cg_ref, cv_ref, o_ref, hg_ref, hv_ref, *, per_seq):
    _zero_history([hg_ref, hv_ref], pl.program_id(1) % per_seq == 0)
    tm = h_ref.shape[0]
    for rows in _row_chunks(tm):
        h = h_ref[rows, :]
        for c, cols in enumerate(_chunks(o_ref.shape[1])):
            ug = _causal_conv3(jnp.dot(h, wg_ref[:, cols], preferred_element_type=F32),
                               cg_ref, cols, hg_ref, c, rows)
            uv = _causal_conv3(jnp.dot(h, wv_ref[:, cols], preferred_element_type=F32),
                               cv_ref, cols, hv_ref, c, rows)
            o_ref[rows, cols] = (ug * _sigmoid(ug) * uv).astype(o_ref.dtype)
    _carry_history([hg_ref, hv_ref], tm)


def _ffn_up(h2, w_up, conv_w, d_ff, seq):
    t, d = h2.shape
    tm = _tile(seq, 1024)
    tn = _tile(d_ff, 512)
    nj = d_ff // tn
    kw = conv_w.shape[0]
    return pl.pallas_call(
        functools.partial(_ffn_up_kernel, per_seq=seq // tm),
        grid=(nj, t // tm),
        in_specs=[pl.BlockSpec((tm, d), lambda j, i: (i, 0)),
                  pl.BlockSpec((d, tn), lambda j, i: (0, j)),
                  pl.BlockSpec((d, tn), lambda j, i: (0, nj + j)),
                  pl.BlockSpec((kw, tn), lambda j, i: (0, j)),
                  pl.BlockSpec((kw, tn), lambda j, i: (0, nj + j))],
        out_specs=pl.BlockSpec((tm, tn), lambda j, i: (i, j)),
        out_shape=jax.ShapeDtypeStruct((t, d_ff), BF16),
        scratch_shapes=[_hist_scratch(tm, tn), _hist_scratch(tm, tn)],
        compiler_params=_params(("arbitrary", "arbitrary")),
        name="ffn_up",
    )(h2, w_up, w_up, conv_w, conv_w)


def kernel(x, c, positions, w_ada, b_ada, g_mix, w_in, conv_w, w_attn_out, w_conv_out,
           w_o, g_ffn, w_up, ffn_conv_w, w_down, g_final):
    b, s, d = x.shape
    depth = w_ada.shape[0]
    a = w_attn_out.shape[1]
    cw = w_conv_out.shape[1]
    d_ff = w_down.shape[1]
    n_heads = a // HEAD_DIM
    assert conv_w.shape[1] == 3 and ffn_conv_w.shape[1] == 3
    assert s % MOBA_BLOCK == 0 and w_in.shape[2] == 3 * a + 3 * cw + 2 * d

    xt = x.reshape(b * s, d)
    cos_t, sin_t = _rope_tables(positions)
    for layer in range(depth):
        mod3 = _ada(c, w_ada[layer], b_ada[layer]).reshape(b * N_MOD, 1, d)
        w_in_l = w_in[layer].astype(BF16)

        h = _normmod(xt, g_mix[layer], mod3, 0, 1, s)
        qk = _qk(h, w_in_l, cos_t, sin_t, a)
        v = _proj(h, w_in_l, 2 * a, a, False, "v_proj")
        z = _convbranch(h, w_in_l, conv_w[layer], 3 * a, cw, s)
        gates = _proj(h, w_in_l, 3 * a + 3 * cw, 2 * d, True, "gates")
        attn = _moba(qk, v, b, s, n_heads)
        merged = _merge(attn, z, w_attn_out[layer].astype(BF16), w_conv_out[layer].astype(BF16), gates)
        xt = _resid_full(merged, w_o[layer].astype(BF16), xt, mod3, 2, s)

        h2 = _normmod(xt, g_ffn[layer], mod3, 3, 4, s)
        act = _ffn_up(h2, w_up[layer].astype(BF16), ffn_conv_w[layer], d_ff, s)
        xt = _resid_ktiled(act, w_down[layer].astype(BF16), xt, mod3, 5, s, 3584)
    return _final_norm(xt, g_final).reshape(b, s, d)
```

```python
import functools
import math

import jax
import jax.numpy as jnp
from jax import lax
from jax.experimental import pallas as pl
from jax.experimental.pallas import tpu as pltpu

F32 = jnp.float32
BF16 = jnp.bfloat16

HEAD_DIM = 128
ROT_DIM = HEAD_DIM // 4
ROPE_THETA = 500000.0
MOBA_BLOCK = 256
MOBA_TOPK = 3
EPS = 1e-6
N_MOD = 6
LANES = 128
MXU_COLS = 256
ROW_CHUNK = 256
HALO = 8
VMEM_LIMIT = 56 * 1024 * 1024
LOG2E = math.log2(math.e)


def _tile(dim, target):
    if dim <= target:
        return dim
    t = (target // LANES) * LANES
    while t >= LANES:
        if dim % t == 0:
            return t
        t -= LANES
    return dim


def _chunks(tn):
    w = MXU_COLS if tn % MXU_COLS == 0 else tn
    return [slice(c * w, (c + 1) * w) for c in range(tn // w)]


def _row_chunks(tm):
    r = ROW_CHUNK if tm % ROW_CHUNK == 0 else tm
    return [slice(k * r, (k + 1) * r) for k in range(tm // r)]


def _params(sem):
    return pltpu.CompilerParams(dimension_semantics=sem, vmem_limit_bytes=VMEM_LIMIT)


def _sigmoid(x):
    return 1.0 / (1.0 + jnp.exp2(x * (-LOG2E)))


def _ada_kernel(c_ref, w_ref, b_ref, o_ref):
    c = c_ref[...]
    c_act = c * _sigmoid(c)
    acc = jnp.dot(c_act.astype(BF16), w_ref[...].astype(BF16), preferred_element_type=F32)
    o_ref[...] = acc + b_ref[...]


def _ada(c, w, bias):
    b, d = c.shape
    n = w.shape[1]
    tn = _tile(n, 512)
    return pl.pallas_call(
        _ada_kernel,
        grid=(n // tn,),
        in_specs=[pl.BlockSpec((b, d), lambda j: (0, 0)),
                  pl.BlockSpec((d, tn), lambda j: (0, j)),
                  pl.BlockSpec((1, tn), lambda j: (0, j))],
        out_specs=pl.BlockSpec((b, tn), lambda j: (0, j)),
        out_shape=jax.ShapeDtypeStruct((b, n), F32),
        compiler_params=_params(("arbitrary",)),
        name="ada",
    )(c, w, bias.reshape(1, n))


def _rope_kernel(pos_ref, invf_ref, sign_ref, cos_ref, sin_ref):
    ang = pos_ref[...].astype(F32) * invf_ref[...]
    cos_ref[...] = jnp.cos(ang)
    sin_ref[...] = jnp.sin(ang) * sign_ref[...]


def _rope_tables(positions):
    t = positions.size
    half = ROT_DIM // 2
    inv_freq = ROPE_THETA ** (-jnp.arange(0, ROT_DIM, 2, dtype=F32) / ROT_DIM)
    invf = jnp.zeros((HEAD_DIM,), F32).at[:ROT_DIM].set(jnp.tile(inv_freq, 2)).reshape(1, HEAD_DIM)
    sign = jnp.ones((HEAD_DIM,), F32).at[:half].set(-1.0).reshape(1, HEAD_DIM)
    tm = _tile(t, 2048)
    return pl.pallas_call(
        _rope_kernel,
        grid=(t // tm,),
        in_specs=[pl.BlockSpec((tm, 1), lambda i: (i, 0)),
                  pl.BlockSpec((1, HEAD_DIM), lambda i: (0, 0)),
                  pl.BlockSpec((1, HEAD_DIM), lambda i: (0, 0))],
        out_specs=[pl.BlockSpec((tm, HEAD_DIM), lambda i: (i, 0)),
                   pl.BlockSpec((tm, HEAD_DIM), lambda i: (i, 0))],
        out_shape=[jax.ShapeDtypeStruct((t, HEAD_DIM), F32)] * 2,
        compiler_params=_params(("arbitrary",)),
        name="rope_tables",
    )(positions.reshape(t, 1), invf, sign)


def _normmod_kernel(x_ref, g_ref, shift_ref, scale_ref, o_ref):
    x = x_ref[...]
    y = x * lax.rsqrt(jnp.mean(x * x, axis=-1, keepdims=True) + EPS) * g_ref[...]
    o_ref[...] = (y * (1.0 + scale_ref[0]) + shift_ref[0]).astype(o_ref.dtype)


def _normmod(x2d, g, mod3, shift_idx, scale_idx, seq):
    t, d = x2d.shape
    tm = _tile(seq, 256)
    per = seq // tm
    return pl.pallas_call(
        _normmod_kernel,
        grid=(t // tm,),
        in_specs=[pl.BlockSpec((tm, d), lambda i: (i, 0)),
                  pl.BlockSpec((1, d), lambda i: (0, 0)),
                  pl.BlockSpec((1, 1, d), lambda i: ((i // per) * N_MOD + shift_idx, 0, 0)),
                  pl.BlockSpec((1, 1, d), lambda i: ((i // per) * N_MOD + scale_idx, 0, 0))],
        out_specs=pl.BlockSpec((tm, d), lambda i: (i, 0)),
        out_shape=jax.ShapeDtypeStruct((t, d), BF16),
        compiler_params=_params(("arbitrary",)),
        name="normmod",
    )(x2d, g.reshape(1, d), mod3, mod3)


def _final_norm_kernel(x_ref, g_ref, o_ref):
    x = x_ref[...]
    o_ref[...] = x * lax.rsqrt(jnp.mean(x * x, axis=-1, keepdims=True) + EPS) * g_ref[...]


def _final_norm(x2d, g):
    t, d = x2d.shape
    tm = _tile(t, 256)
    return pl.pallas_call(
        _final_norm_kernel,
        grid=(t // tm,),
        in_specs=[pl.BlockSpec((tm, d), lambda i: (i, 0)),
                  pl.BlockSpec((1, d), lambda i: (0, 0))],
        out_specs=pl.BlockSpec((tm, d), lambda i: (i, 0)),
        out_shape=jax.ShapeDtypeStruct((t, d), F32),
        compiler_params=_params(("arbitrary",)),
        name="final_norm",
    )(x2d, g.reshape(1, d))


def _qk_kernel(h_ref, w_ref, cos_ref, sin_ref, o_ref, *, q_tiles, q_scale):
    fac = jnp.where(pl.program_id(0) < q_tiles, q_scale, 1.0).astype(F32)
    half = ROT_DIM // 2
    for rows in _row_chunks(h_ref.shape[0]):
        h = h_ref[rows, :]
        cos = cos_ref[rows, :] * fac
        sin = sin_ref[rows, :] * fac
        for cols in _chunks(o_ref.shape[1]):
            acc = jnp.dot(h, w_ref[:, cols], preferred_element_type=F32)
            wc = acc.shape[1]
            reps = wc // HEAD_DIM
            lane = lax.broadcasted_iota(jnp.int32, acc.shape, 1) % HEAD_DIM
            partner = jnp.where(lane < half, pltpu.roll(acc, wc - half, 1),
                                pltpu.roll(acc, half, 1))
            o_ref[rows, cols] = (acc * jnp.tile(cos, (1, reps))
                                 + partner * jnp.tile(sin, (1, reps))).astype(o_ref.dtype)


def _qk(h, w_in, cos_t, sin_t, attn_width):
    t, d = h.shape
    n = 2 * attn_width
    tm = _tile(t, 1024)
    tn = _tile(attn_width, 1024)
    kernel = functools.partial(_qk_kernel, q_tiles=attn_width // tn,
                               q_scale=HEAD_DIM ** -0.5 * LOG2E)
    return pl.pallas_call(
        kernel,
        grid=(n // tn, t // tm),
        in_specs=[pl.BlockSpec((tm, d), lambda j, i: (i, 0)),
                  pl.BlockSpec((d, tn), lambda j, i: (0, j)),
                  pl.BlockSpec((tm, HEAD_DIM), lambda j, i: (i, 0)),
                  pl.BlockSpec((tm, HEAD_DIM), lambda j, i: (i, 0))],
        out_specs=pl.BlockSpec((tm, tn), lambda j, i: (i, j)),
        out_shape=jax.ShapeDtypeStruct((t, n), BF16),
        compiler_params=_params(("arbitrary", "arbitrary")),
        name="qk_rope",
    )(h, w_in, cos_t, sin_t)


def _proj_kernel(h_ref, w_ref, o_ref, *, sigmoid):
    for rows in _row_chunks(h_ref.shape[0]):
        h = h_ref[rows, :]
        for cols in _chunks(o_ref.shape[1]):
            acc = jnp.dot(h, w_ref[:, cols], preferred_element_type=F32)
            if sigmoid:
                acc = _sigmoid(acc)
            o_ref[rows, cols] = acc.astype(o_ref.dtype)


def _proj(h, w_in, col0, width, sigmoid, name):
    t, d = h.shape
    tm = _tile(t, 1024)
    tn = _tile(width, 1024)
    assert col0 % tn == 0
    j0 = col0 // tn
    return pl.pallas_call(
        functools.partial(_proj_kernel, sigmoid=sigmoid),
        grid=(width // tn, t // tm),
        in_specs=[pl.BlockSpec((tm, d), lambda j, i: (i, 0)),
                  pl.BlockSpec((d, tn), lambda j, i: (0, j0 + j))],
        out_specs=pl.BlockSpec((tm, tn), lambda j, i: (i, j)),
        out_shape=jax.ShapeDtypeStruct((t, width), BF16),
        compiler_params=_params(("arbitrary", "arbitrary")),
        name=name,
    )(h, w_in)


def _zero_history(hist_refs, first_of_seq):
    @pl.when(first_of_seq)
    def _():
        for ref in hist_refs:
            ref[:, 0:HALO, :] = jnp.zeros((ref.shape[0], HALO, ref.shape[2]), F32)


def _causal_conv3(u, w_ref, cols, hist_ref, c, rows=None):
    r0 = 0 if rows is None else rows.start
    n = u.shape[0]
    hist_ref[c, HALO + r0:HALO + r0 + n, :] = u
    u1 = hist_ref[c, HALO + r0 - 1:HALO + r0 - 1 + n, :]
    u2 = hist_ref[c, HALO + r0 - 2:HALO + r0 - 2 + n, :]
    return u * w_ref[2:3, cols] + u1 * w_ref[1:2, cols] + u2 * w_ref[0:1, cols]


def _carry_history(hist_refs, tm):
    for ref in hist_refs:
        ref[:, 0:HALO, :] = ref[:, tm:tm + HALO, :]


def _hist_scratch(tm, tn):
    chunks = _chunks(tn)
    return pltpu.VMEM((len(chunks), HALO + tm, tn // len(chunks)), F32)


def _convbranch_kernel(h_ref, wb_ref, wc_ref, wu_ref, cw_ref, o_ref, hist_ref, *, per_seq):
    _zero_history([hist_ref], pl.program_id(1) % per_seq == 0)
    h = h_ref[...]
    for c, cols in enumerate(_chunks(o_ref.shape[1])):
        cc = jnp.dot(h, wc_ref[:, cols], preferred_element_type=F32)
        cu = jnp.dot(h, wu_ref[:, cols], preferred_element_type=F32)
        y = _causal_conv3(cc * cu, cw_ref, cols, hist_ref, c)
        cb = jnp.dot(h, wb_ref[:, cols], preferred_element_type=F32)
        o_ref[:, cols] = (cb * y).astype(o_ref.dtype)
    _carry_history([hist_ref], h.shape[0])


def _convbranch(h, w_in, conv_w, col0, conv_width, seq):
    t, d = h.shape
    tm = _tile(seq, 1024)
    tn = _tile(conv_width, 512)
    nj = conv_width // tn
    assert col0 % tn == 0
    b0, c0, u0 = col0 // tn, (col0 + conv_width) // tn, (col0 + 2 * conv_width) // tn
    return pl.pallas_call(
        functools.partial(_convbranch_kernel, per_seq=seq // tm),
        grid=(nj, t // tm),
        in_specs=[pl.BlockSpec((tm, d), lambda j, i: (i, 0)),
                  pl.BlockSpec((d, tn), lambda j, i: (0, b0 + j)),
                  pl.BlockSpec((d, tn), lambda j, i: (0, c0 + j)),
                  pl.BlockSpec((d, tn), lambda j, i: (0, u0 + j)),
                  pl.BlockSpec((conv_w.shape[0], tn), lambda j, i: (0, j))],
        out_specs=pl.BlockSpec((tm, tn), lambda j, i: (i, j)),
        out_shape=jax.ShapeDtypeStruct((t, conv_width), BF16),
        scratch_shapes=[_hist_scratch(tm, tn)],
        compiler_params=_params(("arbitrary", "arbitrary")),
        name="conv_branch",
    )(h, w_in, w_in, w_in, conv_w)


KMEAN_TERMS = 3


def _skewed(n, stages):
    for step in range(n + len(stages) - 1):
        for s, stage in enumerate(stages):
            if 0 <= step - s < n:
                stage(step - s)


def _moba_kernel(q_ref, k_ref, v_ref, o_ref, kmean_ref, vt_ref, sel_ref, *, nb, hp, kc):
    i = pl.program_id(2)
    blk = MOBA_BLOCK
    span = kc * blk
    neg_inf = -jnp.inf
    heads = [slice(hh * HEAD_DIM, (hh + 1) * HEAD_DIM) for hh in range(hp)]
    nt_dims = (((1,), (1,)), ((), ()))

    @pl.when(i == 0)
    def _():
        for hh, hs in enumerate(heads):
            kf = k_ref[:, hs].astype(F32).reshape(nb, blk, HEAD_DIM)
            rest = jnp.sum(kf, axis=1) * (1.0 / blk)
            for term in range(KMEAN_TERMS):
                part = rest.astype(BF16)
                kmean_ref[hh, term * nb:(term + 1) * nb, :] = part
                rest = rest - part.astype(F32)
            for t in range(nb // kc):
                vt_ref[hh, t] = v_ref[t * span:(t + 1) * span, hs].T

    row = lax.broadcasted_iota(jnp.int32, (nb, blk), 0)
    q = [q_ref[:, hs] for hs in heads]
    for hh in range(hp):
        g3 = lax.dot_general(kmean_ref[hh], q[hh], nt_dims, preferred_element_type=F32)
        gate = g3[0:nb]
        for term in range(1, KMEAN_TERMS):
            gate = gate + g3[term * nb:(term + 1) * nb]
        gate = jnp.where(row < i, gate, neg_inf)
        sel = jnp.zeros((nb, blk), F32)
        for _ in range(MOBA_TOPK):
            best = jnp.max(gate, axis=0, keepdims=True)
            hit = (gate == best) & (best > neg_inf)
            first = jnp.min(jnp.where(hit, row, nb), axis=0, keepdims=True)
            pick = row == first
            sel = jnp.where(pick, 1.0, sel)
            gate = jnp.where(pick, neg_inf, gate)
        for n in range(nb):
            sel_ref[hh, n] = sel[n:n + 1, :]

    def scores(hh, start, size):
        kj = k_ref[pl.ds(start, size), heads[hh]]
        return lax.dot_general(kj, q[hh], nt_dims, preferred_element_type=F32)

    own = pl.multiple_of(i * blk, blk)
    kpos = lax.broadcasted_iota(jnp.int32, (blk, blk), 0)
    qpos = lax.broadcasted_iota(jnp.int32, (blk, blk), 1)
    causal = kpos <= qpos
    s_own, p_own, carry = [None] * hp, [None] * hp, [None] * hp

    def own_scores(hh):
        s_own[hh] = jnp.where(causal, scores(hh, own, blk), neg_inf)

    def own_softmax(hh):
        m = jnp.max(s_own[hh], axis=0, keepdims=True)
        p_own[hh] = jnp.exp2(s_own[hh] - m)
        carry[hh] = (m, jnp.sum(p_own[hh], axis=0, keepdims=True))

    def own_pv(hh):
        vj = v_ref[pl.ds(own, blk), heads[hh]]
        acc = lax.dot_general(vj, p_own[hh].astype(vj.dtype), (((0,), (0,)), ((), ())),
                              preferred_element_type=F32)
        carry[hh] = carry[hh] + (acc,)

    _skewed(hp, [own_scores, own_softmax, own_pv])

    def body(t, carry):
        start = pl.multiple_of(t * span, span)
        s_all, p_all, stat, out = [None] * hp, [None] * hp, [None] * hp, [None] * hp

        def past_scores(hh):
            chosen = sel_ref[hh, pl.ds(t * kc, kc)]
            chosen = jnp.broadcast_to(chosen, (kc, blk, blk)).reshape(span, blk)
            s_all[hh] = jnp.where(chosen > 0.0, scores(hh, start, span), neg_inf)

        def past_softmax(hh):
            m, l, _ = carry[hh]
            m_new = jnp.maximum(m, jnp.max(s_all[hh], axis=0, keepdims=True))
            alpha = jnp.exp2(m - m_new)
            p = jnp.exp2(s_all[hh] - m_new)
            p_all[hh] = p.astype(BF16)
            stat[hh] = (m_new, alpha * l + jnp.sum(p, axis=0, keepdims=True), alpha)

        def past_pv(hh):
            m_new, l, alpha = stat[hh]
            acc = alpha * carry[hh][2] + jnp.dot(vt_ref[hh, t], p_all[hh],
                                                 preferred_element_type=F32)
            out[hh] = (m_new, l, acc)

        _skewed(hp, [past_scores, past_softmax, past_pv])
        return tuple(out)

    carry = lax.fori_loop(0, (i + kc - 1) // kc, body, tuple(carry))
    for hh in range(hp):
        _, l, acc = carry[hh]
        o_ref[:, heads[hh]] = (acc / l).T.astype(o_ref.dtype)


def _moba(qk, v, batch, seq, n_heads):
    t = qk.shape[0]
    nb = seq // MOBA_BLOCK
    hp = next(n for n in (8, 4, 2, 1) if n_heads % n == 0)
    kc = 2 if nb % 2 == 0 else 1
    ng = n_heads // hp
    w = hp * HEAD_DIM
    return pl.pallas_call(
        functools.partial(_moba_kernel, nb=nb, hp=hp, kc=kc),
        grid=(batch, ng, nb),
        in_specs=[pl.BlockSpec((MOBA_BLOCK, w), lambda b, g, i: (b * nb + i, g)),
                  pl.BlockSpec((seq, w), lambda b, g, i: (b, ng + g)),
                  pl.BlockSpec((seq, w), lambda b, g, i: (b, g))],
        out_specs=pl.BlockSpec((MOBA_BLOCK, w), lambda b, g, i: (b * nb + i, g)),
        out_shape=jax.ShapeDtypeStruct((t, n_heads * HEAD_DIM), BF16),
        scratch_shapes=[pltpu.VMEM((hp, KMEAN_TERMS * nb, HEAD_DIM), BF16),
                        pltpu.VMEM((hp, nb // kc, HEAD_DIM, kc * MOBA_BLOCK), BF16),
                        pltpu.VMEM((hp, nb, 1, MOBA_BLOCK), F32)],
        compiler_params=_params(("arbitrary", "arbitrary", "arbitrary")),
        name="moba",
    )(qk, qk, v)


def _merge_kernel(a_ref, z_ref, wa_ref, wz_ref, ga_ref, gz_ref, o_ref):
    a = a_ref[...]
    z = z_ref[...]
    for cols in _chunks(o_ref.shape[1]):
        ya = jnp.dot(a, wa_ref[:, cols], preferred_element_type=F32)
        yz = jnp.dot(z, wz_ref[:, cols], preferred_element_type=F32)
        o_ref[:, cols] = (ga_ref[:, cols].astype(F32) * ya
                          + gz_ref[:, cols].astype(F32) * yz).astype(o_ref.dtype)


def _merge(attn, z, w_attn_out, w_conv_out, gates):
    t, ka = attn.shape
    kz = z.shape[1]
    n = w_attn_out.shape[1]
    tm = _tile(t, 1024)
    tn = _tile(n, 1024)
    nj = n // tn
    return pl.pallas_call(
        _merge_kernel,
        grid=(nj, t // tm),
        in_specs=[pl.BlockSpec((tm, ka), lambda j, i: (i, 0)),
                  pl.BlockSpec((tm, kz), lambda j, i: (i, 0)),
                  pl.BlockSpec((ka, tn), lambda j, i: (0, j)),
                  pl.BlockSpec((kz, tn), lambda j, i: (0, j)),
                  pl.BlockSpec((tm, tn), lambda j, i: (i, j)),
                  pl.BlockSpec((tm, tn), lambda j, i: (i, nj + j))],
        out_specs=pl.BlockSpec((tm, tn), lambda j, i: (i, j)),
        out_shape=jax.ShapeDtypeStruct((t, n), BF16),
        compiler_params=_params(("arbitrary", "arbitrary")),
        name="merge",
    )(attn, z, w_attn_out, w_conv_out, gates, gates)


def _resid_full_kernel(a_ref, w_ref, x_ref, g_ref, o_ref):
    a = a_ref[...]
    for cols in _chunks(o_ref.shape[1]):
        y = jnp.dot(a, w_ref[:, cols], preferred_element_type=F32)
        o_ref[:, cols] = x_ref[:, cols] + g_ref[0, :, cols] * y


def _resid_full(a, w, x2d, mod3, gate_idx, seq):
    t, kdim = a.shape
    n = w.shape[1]
    tm = _tile(seq, 1024)
    tn = _tile(n, 512)
    per = seq // tm
    return pl.pallas_call(
        _resid_full_kernel,
        grid=(n // tn, t // tm),
        in_specs=[pl.BlockSpec((tm, kdim), lambda j, i: (i, 0)),
                  pl.BlockSpec((kdim, tn), lambda j, i: (0, j)),
                  pl.BlockSpec((tm, tn), lambda j, i: (i, j)),
                  pl.BlockSpec((1, 1, tn), lambda j, i: ((i // per) * N_MOD + gate_idx, 0, j))],
        out_specs=pl.BlockSpec((tm, tn), lambda j, i: (i, j)),
        out_shape=jax.ShapeDtypeStruct((t, n), F32),
        compiler_params=_params(("arbitrary", "arbitrary")),
        name="resid_full",
    )(a, w, x2d, mod3)


def _resid_ktiled_kernel(a_ref, w_ref, x_ref, g_ref, o_ref, acc_ref):
    k = pl.program_id(2)

    @pl.when((pl.program_id(0) == 0) & (pl.program_id(1) == 0) & (k == 0))
    def _():
        acc_ref[...] = jnp.zeros(acc_ref.shape, F32)

    for rows in _row_chunks(a_ref.shape[0]):
        a = a_ref[rows, :]
        for cols in _chunks(o_ref.shape[1]):
            part = jnp.dot(a, w_ref[:, cols], preferred_element_type=F32)
            acc = jnp.where(k == 0, 0.0, acc_ref[rows, cols]) + part
            acc_ref[rows, cols] = acc
            o_ref[rows, cols] = x_ref[rows, cols] + g_ref[0, :, cols] * acc


def _resid_ktiled(a, w, x2d, mod3, gate_idx, seq, tk_target):
    t, kdim = a.shape
    n = w.shape[1]
    tm = _tile(seq, 1024)
    tn = _tile(n, 1024)
    tk = _tile(kdim, tk_target)
    per = seq // tm
    return pl.pallas_call(
        _resid_ktiled_kernel,
        grid=(n // tn, t // tm, kdim // tk),
        in_specs=[pl.BlockSpec((tm, tk), lambda j, i, k: (i, k)),
                  pl.BlockSpec((tk, tn), lambda j, i, k: (k, j)),
                  pl.BlockSpec((tm, tn), lambda j, i, k: (i, j)),
                  pl.BlockSpec((1, 1, tn), lambda j, i, k: ((i // per) * N_MOD + gate_idx, 0, j))],
        out_specs=pl.BlockSpec((tm, tn), lambda j, i, k: (i, j)),
        out_shape=jax.ShapeDtypeStruct((t, n), F32),
        scratch_shapes=[pltpu.VMEM((tm, tn), F32)],
        compiler_params=_params(("arbitrary", "arbitrary", "arbitrary")),
        name="resid_ktiled",
    )(a, w, x2d, mod3)


def _ffn_up_kernel(h_ref, wg_ref, wv_ref, cg_ref, cv_ref, o_ref, hg_ref, hv_ref, *, per_seq):
    _zero_history([hg_ref, hv_ref], pl.program_id(1) % per_seq == 0)
    tm = h_ref.shape[0]
    for rows in _row_chunks(tm):
        h = h_ref[rows, :]
        for c, cols in enumerate(_chunks(o_ref.shape[1])):
            ug = _causal_conv3(jnp.dot(h, wg_ref[:, cols], preferred_element_type=F32),
                               cg_ref, cols, hg_ref, c, rows)
            uv = _causal_conv3(jnp.dot(h, wv_ref[:, cols], preferred_element_type=F32),
                               cv_ref, cols, hv_ref, c, rows)
            o_ref[rows, cols] = (ug * _sigmoid(ug) * uv).astype(o_ref.dtype)
    _carry_history([hg_ref, hv_ref], tm)


def _ffn_up(h2, w_up, conv_w, d_ff, seq):
    t, d = h2.shape
    tm = _tile(seq, 2048)
    tn = _tile(d_ff, 512)
    nj = d_ff // tn
    kw = conv_w.shape[0]
    resident = pl.Buffered(1)
    return pl.pallas_call(
        functools.partial(_ffn_up_kernel, per_seq=seq // tm),
        grid=(nj, t // tm),
        in_specs=[pl.BlockSpec((tm, d), lambda j, i: (i, 0)),
                  pl.BlockSpec((d, tn), lambda j, i: (0, j), pipeline_mode=resident),
                  pl.BlockSpec((d, tn), lambda j, i: (0, nj + j), pipeline_mode=resident),
                  pl.BlockSpec((kw, tn), lambda j, i: (0, j)),
                  pl.BlockSpec((kw, tn), lambda j, i: (0, nj + j))],
        out_specs=pl.BlockSpec((tm, tn), lambda j, i: (i, j)),
        out_shape=jax.ShapeDtypeStruct((t, d_ff), BF16),
        scratch_shapes=[_hist_scratch(tm, tn), _hist_scratch(tm, tn)],
        compiler_params=_params(("arbitrary", "arbitrary")),
        name="ffn_up",
    )(h2, w_up, w_up, conv_w, conv_w)


def kernel(x, c, positions, w_ada, b_ada, g_mix, w_in, conv_w, w_attn_out, w_conv_out,
           w_o, g_ffn, w_up, ffn_conv_w, w_down, g_final):
    b, s, d = x.shape
    depth = w_ada.shape[0]
    a = w_attn_out.shape[1]
    cw = w_conv_out.shape[1]
    d_ff = w_down.shape[1]
    n_heads = a // HEAD_DIM
    assert conv_w.shape[1] == 3 and ffn_conv_w.shape[1] == 3
    assert s % MOBA_BLOCK == 0 and w_in.shape[2] == 3 * a + 3 * cw + 2 * d

    xt = x.reshape(b * s, d)
    cos_t, sin_t = _rope_tables(positions)
    for layer in range(depth):
        mod3 = _ada(c, w_ada[layer], b_ada[layer]).reshape(b * N_MOD, 1, d)
        w_in_l = w_in[layer].astype(BF16)

        h = _normmod(xt, g_mix[layer], mod3, 0, 1, s)
        qk = _qk(h, w_in_l, cos_t, sin_t, a)
        v = _proj(h, w_in_l, 2 * a, a, False, "v_proj")
        z = _convbranch(h, w_in_l, conv_w[layer], 3 * a, cw, s)
        gates = _proj(h, w_in_l, 3 * a + 3 * cw, 2 * d, True, "gates")
        attn = _moba(qk, v, b, s, n_heads)
        merged = _merge(attn, z, w_attn_out[layer].astype(BF16), w_conv_out[layer].astype(BF16), gates)
        xt = _resid_full(merged, w_o[layer].astype(BF16), xt, mod3, 2, s)

        h2 = _normmod(xt, g_ffn[layer], mod3, 3, 4, s)
        act = _ffn_up(h2, w_up[layer].astype(BF16), ffn_conv_w[layer], d_ff, s)
        xt = _resid_ktiled(act, w_down[layer].astype(BF16), xt, mod3, 5, s, 3584)
    return _final_norm(xt, g_final).reshape(b, s, d)
```
